```python
import jax, jax.numpy as jnp
from jax import lax
import numpy as np

D_MODEL = 2048
BATCH = 2
SEQ = 16384
DEPTH = 1
DEC_BATCH = 8
DEC_SEQ = 64
PAST_LEN = 1024

CHUNK = 64
HEAD_DIM = 64
D_MIX = D_MODEL
D_A = D_MIX // 2
D_B = D_MIX - D_A
N_HEADS_A = D_A // HEAD_DIM
N_HEADS_B = D_B // HEAD_DIM
N_KV_B = N_HEADS_B // 4
GROUP_B = N_HEADS_B // N_KV_B
KV_B = N_KV_B * HEAD_DIM
PAST_CHUNKS_A = 8
WINDOW_B = 128
PAST_CHUNKS_B = WINDOW_B // CHUNK
PAST_ROWS_A = PAST_CHUNKS_A * CHUNK
PAST_ROWS_B = PAST_CHUNKS_B * CHUNK
REL_CLIP = 256
N_REL = REL_CLIP + CHUNK
SPLIT_SIZES = (D_A, D_A, D_A, D_A, D_B, KV_B, KV_B, D_B)
D_IN = sum(SPLIT_SIZES)
RMS_EPS = 1e-6
NEG_INF = -1e30

kernel_name = 'chunk_relpos_swa_sink_hybrid_step'


def rms_norm(x, g):
    xf = x.astype(jnp.float32)
    y = xf * lax.rsqrt(jnp.mean(xf * xf, axis=-1, keepdims=True) + RMS_EPS)
    return (y * g.astype(jnp.float32)).astype(x.dtype)


def band_distances(past_rows):
    return past_rows + np.arange(CHUNK)[:, None] - np.arange(past_rows + CHUNK)[None, :]


def rel_position_bias(table):
    idx = np.clip(band_distances(PAST_ROWS_A), -(CHUNK - 1), REL_CLIP) + (CHUNK - 1)
    return table.astype(jnp.float32)[:, idx][:, None]


def alibi_bias():
    d = np.abs(band_distances(PAST_ROWS_B)).astype(np.float32)
    slopes = (2.0 ** (-8.0 * np.arange(1, N_HEADS_B + 1, dtype=np.float32) / N_HEADS_B)).astype(np.float32)
    bias = -slopes[:, None, None] * d[None]
    return jnp.asarray(bias.reshape(N_KV_B, GROUP_B, CHUNK, -1), dtype=jnp.float32)


def key_validity(past_rows, n_hist, n_new, n_rows):
    idx = np.arange(n_rows)
    return jnp.asarray(((idx >= past_rows - n_hist) & (idx < past_rows + n_new))[None, :])


def chunk_band_attention(q, k_full, v_full, key_valid, bias, sinks):
    band = bias.shape[-1]
    n_chunks = q.shape[1] // CHUNK
    scale = HEAD_DIM ** -0.5

    def one_chunk(c):
        start = c * CHUNK
        qc = lax.dynamic_slice_in_dim(q, start, CHUNK, axis=1)
        kc = lax.dynamic_slice_in_dim(k_full, start, band, axis=1)
        vc = lax.dynamic_slice_in_dim(v_full, start, band, axis=1)
        ok = lax.dynamic_slice_in_dim(key_valid, start, band, axis=1)
        s = jnp.einsum('bqhgd,bkhd->bhgqk', qc, kc).astype(jnp.float32) * scale + bias
        s = jnp.where(ok[:, None, None, None, :], s, NEG_INF)
        if sinks is None:
            p = jax.nn.softmax(s, axis=-1)
        else:
            sink = sinks.astype(jnp.float32)[None, :, :, None, None]
            m = jnp.maximum(jnp.max(s, axis=-1, keepdims=True), sink)
            e = jnp.exp(s - m)
            p = e / (jnp.sum(e, axis=-1, keepdims=True) + jnp.exp(sink - m))
        return jnp.einsum('bhgqk,bkhd->bqhgd', p.astype(vc.dtype), vc)

    out = lax.map(one_chunk, jnp.arange(n_chunks))
    return jnp.moveaxis(out, 0, 1).reshape(q.shape)


def mixer_layer(x, hist_ak, hist_av, hist_bk, hist_bv, valid_a, valid_b,
                norm_g, w_in, rel_table, sinks, w_out):
    b, s, _ = x.shape
    h = rms_norm(x, norm_g)
    z = jnp.einsum('bsd,de->bse', h, w_in)
    offsets = list(np.cumsum(SPLIT_SIZES)[:-1])
    qa, ka, va, ga, qb, kb, vb, gb = jnp.split(z, offsets, axis=-1)
    qa = qa.reshape(b, s, N_HEADS_A, 1, HEAD_DIM)
    ka = ka.reshape(b, s, N_HEADS_A, HEAD_DIM)
    va = va.reshape(b, s, N_HEADS_A, HEAD_DIM)
    qb = qb.reshape(b, s, N_KV_B, GROUP_B, HEAD_DIM)
    kb = kb.reshape(b, s, N_KV_B, HEAD_DIM)
    vb = vb.reshape(b, s, N_KV_B, HEAD_DIM)
    oa = chunk_band_attention(qa, jnp.concatenate([hist_ak, ka], axis=1),
                              jnp.concatenate([hist_av, va], axis=1),
                              valid_a, rel_position_bias(rel_table), None)
    ob = chunk_band_attention(qb, jnp.concatenate([hist_bk, kb], axis=1),
                              jnp.concatenate([hist_bv, vb], axis=1),
                              valid_b, alibi_bias(), sinks.reshape(N_KV_B, GROUP_B))
    o = jnp.concatenate([oa.reshape(b, s, D_A) * jax.nn.silu(ga),
                         ob.reshape(b, s, D_B) * jax.nn.silu(gb)], axis=-1)
    y = x + jnp.einsum('bse,ed->bsd', o, w_out)
    return y, ka, va, kb, vb


def setup_inputs(seed: int = 0) -> dict:
    key = jax.random.key(seed)
    ks = jax.random.split(key, 12)
    la = min(PAST_ROWS_A, PAST_LEN)
    lb = min(PAST_ROWS_B, PAST_LEN)
    f32 = jnp.float32
    return {
        'x_prompt': jax.random.normal(ks[0], (BATCH, SEQ, D_MODEL), f32),
        'x_sample': jax.random.normal(ks[1], (DEC_BATCH, DEC_SEQ, D_MODEL), f32),
        'cache_a_k': jax.random.normal(ks[2], (DEPTH, DEC_BATCH, la, N_HEADS_A, HEAD_DIM), f32),
        'cache_a_v': jax.random.normal(ks[3], (DEPTH, DEC_BATCH, la, N_HEADS_A, HEAD_DIM), f32),
        'cache_b_k': jax.random.normal(ks[4], (DEPTH, DEC_BATCH, lb, N_KV_B, HEAD_DIM), f32),
        'cache_b_v': jax.random.normal(ks[5], (DEPTH, DEC_BATCH, lb, N_KV_B, HEAD_DIM), f32),
        'norm_in': 1.0 + 0.05 * jax.random.normal(ks[6], (DEPTH, D_MODEL), f32),
        'w_in': jax.random.normal(ks[7], (DEPTH, D_MODEL, D_IN), f32) * D_MODEL ** -0.5,
        'rel_bias_a': 0.2 * jax.random.normal(ks[8], (DEPTH, N_HEADS_A, N_REL), f32),
        'sinks_b': jax.random.normal(ks[9], (DEPTH, N_HEADS_B), f32),
        'w_out': jax.random.normal(ks[10], (DEPTH, D_MIX, D_MODEL), f32) * D_MIX ** -0.5,
        'norm_final': 1.0 + 0.05 * jax.random.normal(ks[11], (D_MODEL,), f32),
    }


def reference(x_prompt, x_sample, cache_a_k, cache_a_v, cache_b_k, cache_b_v,
              norm_in, w_in, rel_bias_a, sinks_b, w_out, norm_final):
    b_p, seq, _ = x_prompt.shape
    b_s, n_new, _ = x_sample.shape
    la = cache_a_k.shape[2]
    lb = cache_b_k.shape[2]
    s_pad = -(-n_new // CHUNK) * CHUNK
    keep_a = min(PAST_ROWS_A, seq)
    keep_b = min(PAST_ROWS_B, seq)

    valid_a_p = key_validity(PAST_ROWS_A, 0, seq, PAST_ROWS_A + seq)
    valid_b_p = key_validity(PAST_ROWS_B, 0, seq, PAST_ROWS_B + seq)
    valid_a_s = key_validity(PAST_ROWS_A, la, n_new, PAST_ROWS_A + s_pad)
    valid_b_s = key_validity(PAST_ROWS_B, lb, n_new, PAST_ROWS_B + s_pad)

    hp = x_prompt
    hs = jnp.pad(x_sample, ((0, 0), (0, s_pad - n_new), (0, 0)))
    ak_p, av_p, bk_p, bv_p = [], [], [], []
    ak_s, av_s, bk_s, bv_s = [], [], [], []
    for l in range(DEPTH):
        za = jnp.zeros((b_p, PAST_ROWS_A, N_HEADS_A, HEAD_DIM), hp.dtype)
        zb = jnp.zeros((b_p, PAST_ROWS_B, N_KV_B, HEAD_DIM), hp.dtype)
        hp, ka, va, kb, vb = mixer_layer(hp, za, za, zb, zb, valid_a_p, valid_b_p,
                                         norm_in[l], w_in[l], rel_bias_a[l], sinks_b[l], w_out[l])
        ak_p.append(ka[:, seq - keep_a:])
        av_p.append(va[:, seq - keep_a:])
        bk_p.append(kb[:, seq - keep_b:])
        bv_p.append(vb[:, seq - keep_b:])
        pad_a = ((0, 0), (PAST_ROWS_A - la, 0), (0, 0), (0, 0))
        pad_b = ((0, 0), (PAST_ROWS_B - lb, 0), (0, 0), (0, 0))
        hs, ka, va, kb, vb = mixer_layer(hs, jnp.pad(cache_a_k[l], pad_a), jnp.pad(cache_a_v[l], pad_a),
                                         jnp.pad(cache_b_k[l], pad_b), jnp.pad(cache_b_v[l], pad_b),
                                         valid_a_s, valid_b_s,
                                         norm_in[l], w_in[l], rel_bias_a[l], sinks_b[l], w_out[l])
        ak_s.append(jnp.concatenate([cache_a_k[l], ka[:, :n_new]], axis=1)[:, n_new:])
        av_s.append(jnp.concatenate([cache_a_v[l], va[:, :n_new]], axis=1)[:, n_new:])
        bk_s.append(jnp.concatenate([cache_b_k[l], kb[:, :n_new]], axis=1)[:, n_new:])
        bv_s.append(jnp.concatenate([cache_b_v[l], vb[:, :n_new]], axis=1)[:, n_new:])

    y_prompt = rms_norm(hp, norm_final)
    y_sample = rms_norm(hs[:, :n_new], norm_final)
    return (y_prompt, y_sample,
            jnp.stack(ak_p), jnp.stack(av_p), jnp.stack(bk_p), jnp.stack(bv_p),
            jnp.stack(ak_s), jnp.stack(av_s), jnp.stack(bk_s), jnp.stack(bv_s))
```

```python
import functools

import jax
import jax.numpy as jnp
import numpy as np
from jax import lax
from jax.experimental import pallas as pl
from jax.experimental.pallas import tpu as pltpu

CHUNK = 64
HEAD_DIM = 64
N_HEADS = 16
N_KV_B = 4
GROUP_B = N_HEADS // N_KV_B
D_A = N_HEADS * HEAD_DIM
D_B = N_HEADS * HEAD_DIM
KV_B = N_KV_B * HEAD_DIM
HIST_A = 8 * CHUNK
HIST_B = 2 * CHUNK
REL_CLIP = 256
N_REL = REL_CLIP + CHUNK
RMS_EPS = 1e-6
NEG_INF = -1e30
QA, KA, VA, GA = 0, D_A, 2 * D_A, 3 * D_A
QB = 4 * D_A
KB = QB + D_B
VB = KB + KV_B
GB = VB + KV_B
D_IN = GB + D_B

LANES = 128
VMEM_LIMIT_BYTES = 56 * 1024 * 1024

BF16 = jnp.bfloat16
F32 = jnp.float32


def _resident(shape):
    return pl.BlockSpec(shape, lambda *_: (0,) * len(shape), pipeline_mode=pl.Buffered(1))


def _rel_bias_kernel(tab_ref, out_ref, *, group, width):
    band = out_ref.shape[1]
    n = lax.broadcasted_iota(jnp.int32, (N_REL, width), 0)
    u = lax.broadcasted_iota(jnp.int32, (N_REL, width), 1)
    idx = jnp.clip(HIST_A + group - 1 - u, -(CHUNK - 1), REL_CLIP) + (CHUNK - 1)
    onehot = (n == idx).astype(F32)
    gv = jnp.dot(tab_ref[...], onehot, precision=lax.Precision.HIGHEST, preferred_element_type=F32)
    r = lax.broadcasted_iota(jnp.int32, (group, band), 0)
    j = lax.broadcasted_iota(jnp.int32, (group, band), 1)
    rel = j - (r // CHUNK) * CHUNK
    valid = (rel >= 0) & (rel < HIST_A + CHUNK)
    for h in range(N_HEADS):
        rows = jnp.broadcast_to(gv[h:h + 1, :], (group, width))
        rows = pltpu.roll(rows, width - group + 1, 1, stride=1, stride_axis=0)
        out_ref[h * group:(h + 1) * group, :] = jnp.where(valid, rows[:, :band], NEG_INF)


def _rel_bias(table, group):
    band = HIST_A + group
    width = -(-(HIST_A + 2 * group - 1) // LANES) * LANES
    return pl.pallas_call(
        functools.partial(_rel_bias_kernel, group=group, width=width),
        out_shape=jax.ShapeDtypeStruct((N_HEADS * group, band), F32),
        name=f"rel_bias_g{group}",
    )(table)


def _alibi_bias(group):
    r = np.arange(group)[:, None]
    j = np.arange(HIST_B + group)[None, :]
    d = np.abs(HIST_B + r - j).astype(np.float32)
    slopes = (2.0 ** (-8.0 * np.arange(1, N_HEADS + 1, dtype=np.float32) / N_HEADS)).astype(np.float32)
    bias = -slopes[:, None, None] * d[None]
    rel = j - (r // CHUNK) * CHUNK
    valid = (rel >= 0) & (rel < HIST_B + CHUNK)
    bias = np.where(valid[None], bias, np.float32(NEG_INF)).astype(np.float32)
    return jnp.asarray(bias.reshape(N_HEADS * group, HIST_B + group))


def _inproj_kernel(x_ref, g_ref, w_ref, z_ref, h_ref, *, tn):
    x = x_ref[...]
    ms = jnp.mean(x * x, axis=-1, keepdims=True)
    h_ref[...] = (x * lax.rsqrt(ms + RMS_EPS) * g_ref[...]).astype(h_ref.dtype)
    for n0 in range(0, z_ref.shape[1], tn):
        z_ref[:, n0:n0 + tn] = jnp.dot(
            h_ref[...], w_ref[:, n0:n0 + tn], preferred_element_type=F32).astype(z_ref.dtype)


def _inproj(x2d, gain, w_bf16, out_dtype, tm):
    m, d = x2d.shape
    return pl.pallas_call(
        functools.partial(_inproj_kernel, tn=512),
        grid=(m // tm,),
        in_specs=[
            pl.BlockSpec((tm, d), lambda i: (i, 0)),
            _resident((1, d)),
            _resident((d, D_IN)),
        ],
        out_specs=pl.BlockSpec((tm, D_IN), lambda i: (i, 0)),
        out_shape=jax.ShapeDtypeStruct((m, D_IN), out_dtype),
        scratch_shapes=[pltpu.VMEM((tm, d), BF16)],
        compiler_params=pltpu.CompilerParams(
            dimension_semantics=("arbitrary",), vmem_limit_bytes=VMEM_LIMIT_BYTES),
        name=f"inproj_m{m}",
    )(x2d, gain, w_bf16)


def _dup_halves(x):
    lane = lax.broadcasted_iota(jnp.int32, (x.shape[0], LANES), 1)
    out = []
    for blk in range(x.shape[1] // LANES):
        xb = x[:, blk * LANES:(blk + 1) * LANES]
        rolled = pltpu.roll(xb, HEAD_DIM, 1)
        out.append(jnp.where(lane < HEAD_DIM, xb, rolled))
        out.append(jnp.where(lane < HEAD_DIM, rolled, xb))
    return jnp.concatenate(out, axis=1).astype(BF16)


def _split_heads(q):
    lane = lax.broadcasted_iota(jnp.int32, q.shape, 1)
    zero = jnp.zeros_like(q)
    return [jnp.where(lane < HEAD_DIM, q, zero), jnp.where(lane >= HEAD_DIM, q, zero)]


def _mixer_kernel(*refs, tq, group, n_hist_a, n_hist_b, has_cache, n_tiles):
    if has_cache:
        (z_ref, x_ref, cak_ref, cav_ref, cbk_ref, cbv_ref, bias_a_ref, bias_b_ref, sinks_ref,
         wout_ref, gfin_ref, y_ref, ka_s, va_s, kb_s, vb_s, o_s, y_s) = refs
    else:
        (z_ref, x_ref, bias_a_ref, bias_b_ref, sinks_ref,
         wout_ref, gfin_ref, y_ref, ka_s, va_s, kb_s, vb_s, o_s, y_s) = refs
    t = pl.program_id(1)
    scale = HEAD_DIM ** -0.5

    @pl.when(t == 0)
    def _init_history():
        if has_cache:
            ka_s[0:HIST_A, :] = cak_ref[0].astype(BF16)
            va_s[0:HIST_A, :] = cav_ref[0].astype(BF16)
            kb_s[0:HIST_B, :] = _dup_halves(cbk_ref[0])
            vb_s[0:HIST_B, :] = _dup_halves(cbv_ref[0])
        else:
            ka_s[0:HIST_A, :] = jnp.zeros((HIST_A, D_A), BF16)
            va_s[0:HIST_A, :] = jnp.zeros((HIST_A, D_A), BF16)
            kb_s[0:HIST_B, :] = jnp.zeros((HIST_B, 2 * KV_B), BF16)
            vb_s[0:HIST_B, :] = jnp.zeros((HIST_B, 2 * KV_B), BF16)

    ka_s[HIST_A:HIST_A + tq, :] = z_ref[0, :, KA:KA + D_A].astype(BF16)
    va_s[HIST_A:HIST_A + tq, :] = z_ref[0, :, VA:VA + D_A].astype(BF16)
    kb_s[HIST_B:HIST_B + tq, :] = _dup_halves(z_ref[0, :, KB:KB + KV_B].astype(F32))
    vb_s[HIST_B:HIST_B + tq, :] = _dup_halves(z_ref[0, :, VB:VB + KV_B].astype(F32))

    lane = lax.broadcasted_iota(jnp.int32, (group, LANES), 1)

    def load_q(r0, c0):
        return z_ref[0, pl.ds(r0, group), c0:c0 + LANES].astype(BF16) * scale

    def store_gated(r0, c_out, c_gate, o):
        g = z_ref[0, pl.ds(r0, group), c_gate:c_gate + LANES].astype(F32)
        o_s[pl.ds(r0, group), c_out:c_out + LANES] = (o * (g * jax.nn.sigmoid(g))).astype(BF16)

    def group_body(sg, carry):
        r0 = pl.multiple_of(sg * group, group)
        first = t * tq + r0
        band_a = HIST_A + group
        band_b = HIST_B + group
        if n_hist_a < HIST_A:
            col = lax.broadcasted_iota(jnp.int32, (2 * group, band_a), 1)
            ok_a = col >= HIST_A - n_hist_a - first
        if n_hist_b < HIST_B:
            col = lax.broadcasted_iota(jnp.int32, (GROUP_B * group, band_b), 1)
            ok_b = col >= HIST_B - n_hist_b - first

        for p in range(N_HEADS // 2):
            c0 = p * LANES
            q2 = jnp.concatenate(_split_heads(load_q(r0, QA + c0)), axis=0)
            k = ka_s[pl.ds(r0, band_a), c0:c0 + LANES]
            s = lax.dot_general(q2, k, (((1,), (1,)), ((), ())), preferred_element_type=F32)
            s = s + bias_a_ref[2 * p * group:(2 * p + 2) * group, :]
            if n_hist_a < HIST_A:
                s = jnp.where(ok_a, s, NEG_INF)
            m = jnp.max(s, axis=-1, keepdims=True)
            e = jnp.exp(s - m)
            denom = jnp.sum(e, axis=-1, keepdims=True)
            v = va_s[pl.ds(r0, band_a), c0:c0 + LANES]
            pv = jnp.dot(e.astype(BF16), v, preferred_element_type=F32) / denom
            o = jnp.where(lane < HEAD_DIM, pv[:group], pv[group:])
            store_gated(r0, c0, GA + c0, o)

        for g in range(N_KV_B):
            q4 = []
            for lb in range(2):
                q4 += _split_heads(load_q(r0, QB + g * GROUP_B * HEAD_DIM + lb * LANES))
            q4 = jnp.concatenate(q4, axis=0)
            k = kb_s[pl.ds(r0, band_b), g * LANES:(g + 1) * LANES]
            s = lax.dot_general(q4, k, (((1,), (1,)), ((), ())), preferred_element_type=F32)
            s = s + bias_b_ref[GROUP_B * g * group:GROUP_B * (g + 1) * group, :]
            if n_hist_b < HIST_B:
                s = jnp.where(ok_b, s, NEG_INF)
            sink = jnp.concatenate(
                [jnp.full((group, 1), sinks_ref[GROUP_B * g + i], F32) for i in range(GROUP_B)], axis=0)
            m = jnp.maximum(jnp.max(s, axis=-1, keepdims=True), sink)
            e = jnp.exp(s - m)
            denom = jnp.sum(e, axis=-1, keepdims=True) + jnp.exp(sink - m)
            v = vb_s[pl.ds(r0, band_b), g * LANES:(g + 1) * LANES]
            pv = jnp.dot(e.astype(BF16), v, preferred_element_type=F32) / denom
            for lb in range(2):
                top = pv[2 * lb * group:(2 * lb + 1) * group]
                bot = pv[(2 * lb + 1) * group:(2 * lb + 2) * group]
                c = g * GROUP_B * HEAD_DIM + lb * LANES
                store_gated(r0, D_A + c, GB + c, jnp.where(lane < HEAD_DIM, top, bot))
        return carry

    lax.fori_loop(0, tq // group, group_body, 0)

    d_model = y_s.shape[1]
    tn = 512
    for n0 in range(0, d_model, tn):
        y_s[:, n0:n0 + tn] = x_ref[0, :, n0:n0 + tn] + jnp.dot(
            o_s[...], wout_ref[:, n0:n0 + tn], preferred_element_type=F32)
    y = y_s[...]
    ms = jnp.mean(y * y, axis=-1, keepdims=True)
    y_ref[0] = y * lax.rsqrt(ms + RMS_EPS) * gfin_ref[...]

    if n_tiles > 1:
        for buf, hist in ((ka_s, HIST_A), (va_s, HIST_A), (kb_s, HIST_B), (vb_s, HIST_B)):
            step = min(tq, hist)
            for d0 in range(0, hist, step):
                buf[d0:d0 + step, :] = buf[d0 + tq:d0 + tq + step, :]


def _mixer(z, x, caches, bias_a, bias_b, sinks, wout_bf16, gfin, tq, group):
    nb, s, d = x.shape
    n_tiles = s // tq
    has_cache = caches is not None
    seq_block = lambda w: pl.BlockSpec((1, tq, w), lambda b, t: (b, t, 0))
    in_specs = [seq_block(D_IN), seq_block(d)]
    args = [z, x]
    if has_cache:
        for c in caches:
            in_specs.append(pl.BlockSpec((1,) + c.shape[1:], lambda b, t: (b, 0, 0)))
            args.append(c)
    in_specs += [
        _resident(bias_a.shape),
        _resident(bias_b.shape),
        pl.BlockSpec(memory_space=pltpu.SMEM),
        _resident(wout_bf16.shape),
        _resident(gfin.shape),
    ]
    args += [bias_a, bias_b, sinks, wout_bf16, gfin]
    kern = functools.partial(
        _mixer_kernel, tq=tq, group=group,
        n_hist_a=HIST_A if has_cache else 0, n_hist_b=HIST_B if has_cache else 0,
        has_cache=has_cache, n_tiles=n_tiles)
    return pl.pallas_call(
        kern,
        grid=(nb, n_tiles),
        in_specs=in_specs,
        out_specs=seq_block(d),
        out_shape=jax.ShapeDtypeStruct((nb, s, d), F32),
        scratch_shapes=[
            pltpu.VMEM((HIST_A + tq, D_A), BF16),
            pltpu.VMEM((HIST_A + tq, D_A), BF16),
            pltpu.VMEM((HIST_B + tq, 2 * KV_B), BF16),
            pltpu.VMEM((HIST_B + tq, 2 * KV_B), BF16),
            pltpu.VMEM((tq, D_A + D_B), BF16),
            pltpu.VMEM((tq, d), F32),
        ],
        compiler_params=pltpu.CompilerParams(
            dimension_semantics=("arbitrary", "arbitrary"), vmem_limit_bytes=VMEM_LIMIT_BYTES),
        name=f"mixer_tq{tq}",
    )(*args)


def kernel(x_prompt, x_sample, cache_a_k, cache_a_v, cache_b_k, cache_b_v, norm_in, w_in, rel_bias_a,
           sinks_b, w_out, norm_final):
    b_p, seq, d = x_prompt.shape
    b_s, n_new, _ = x_sample.shape
    assert norm_in.shape[0] == 1, "one layer: the final norm is fused into the mixer call"
    assert n_new == CHUNK and seq % 256 == 0 and seq >= HIST_A
    assert cache_a_k.shape[2] == HIST_A and cache_b_k.shape[2] == HIST_B

    w_in_b = w_in[0].astype(BF16)
    w_out_b = w_out[0].astype(BF16)
    g_in = norm_in[0].reshape(1, d)
    g_fin = norm_final.reshape(1, d)
    sinks = sinks_b[0]

    z_p = _inproj(x_prompt.reshape(b_p * seq, d), g_in, w_in_b, BF16, tm=512).reshape(b_p, seq, D_IN)
    y_p = _mixer(z_p, x_prompt, None, _rel_bias(rel_bias_a[0], 128), _alibi_bias(128), sinks,
                 w_out_b, g_fin, tq=256, group=128)

    z_s = _inproj(x_sample.reshape(b_s * n_new, d), g_in, w_in_b, F32, tm=b_s * n_new)
    z_s = z_s.reshape(b_s, n_new, D_IN)
    caches = (cache_a_k[0].reshape(b_s, HIST_A, D_A), cache_a_v[0].reshape(b_s, HIST_A, D_A),
              cache_b_k[0].reshape(b_s, HIST_B, KV_B), cache_b_v[0].reshape(b_s, HIST_B, KV_B))
    y_s = _mixer(z_s, x_sample, caches, _rel_bias(rel_bias_a[0], CHUNK), _alibi_bias(CHUNK), sinks,
                 w_out_b, g_fin, tq=CHUNK, group=CHUNK)

    def tail(z, c0, heads, keep):
        return z[:, seq - keep:, c0:c0 + heads * HEAD_DIM].astype(F32).reshape(1, b_p, keep, heads, HEAD_DIM)

    def rolled(cache, z, c0, heads):
        new = z[:, :, c0:c0 + heads * HEAD_DIM].reshape(b_s, n_new, heads, HEAD_DIM)
        return jnp.concatenate([cache[0], new], axis=1)[None, :, n_new:]

    return (y_p, y_s,
            tail(z_p, KA, N_HEADS, HIST_A), tail(z_p, VA, N_HEADS, HIST_A),
            tail(z_p, KB, N_KV_B, HIST_B), tail(z_p, VB, N_KV_B, HIST_B),
            rolled(cache_a_k, z_s, KA, N_HEADS), rolled(cache_a_v, z_s, VA, N_HEADS),
            rolled(cache_b_k, z_s, KB, N_KV_B), rolled(cache_b_v, z_s, VB, N_KV_B))
```

```python
import functools

import jax
import jax.numpy as jnp
import numpy as np
from jax import lax
from jax.experimental import pallas as pl
from jax.experimental.pallas import tpu as pltpu

CHUNK = 64
HEAD_DIM = 64
N_HEADS = 16
N_KV_B = 4
GROUP_B = N_HEADS // N_KV_B
D_A = N_HEADS * HEAD_DIM
D_B = N_HEADS * HEAD_DIM
KV_B = N_KV_B * HEAD_DIM
HIST_A = 8 * CHUNK
HIST_B = 2 * CHUNK
REL_CLIP = 256
N_REL = REL_CLIP + CHUNK
RMS_EPS = 1e-6
NEG_INF = -1e30
LOG2E = 1.4426950408889634

LANES = 128
N_PAIRS = D_A // LANES
QA_C, KA_C, VA_C, GA_C = 0, N_PAIRS, 2 * N_PAIRS, 3 * N_PAIRS
QB_C = 4 * N_PAIRS
KB_C = QB_C + N_PAIRS
VB_C = KB_C + KV_B // LANES
GB_C = VB_C + KV_B // LANES
N_COLS = GB_C + N_PAIRS
D_IN = N_COLS * LANES

VMEM_LIMIT_BYTES = 56 * 1024 * 1024

BF16 = jnp.bfloat16
F32 = jnp.float32
_NT = (((1,), (1,)), ((), ()))


def _resident(shape):
    return pl.BlockSpec(shape, lambda *_: (0,) * len(shape), pipeline_mode=pl.Buffered(1))


def _rel_bias_kernel(tab_ref, out_ref, *, group, width):
    band = out_ref.shape[2]
    n = lax.broadcasted_iota(jnp.int32, (N_REL, width), 0)
    u = lax.broadcasted_iota(jnp.int32, (N_REL, width), 1)
    idx = jnp.clip(HIST_A + group - 1 - u, -(CHUNK - 1), REL_CLIP) + (CHUNK - 1)
    onehot = (n == idx).astype(F32)
    gv = jnp.dot(tab_ref[...], onehot, precision=lax.Precision.HIGHEST, preferred_element_type=F32) * LOG2E
    r = lax.broadcasted_iota(jnp.int32, (group, band), 0)
    j = lax.broadcasted_iota(jnp.int32, (group, band), 1)
    rel = j - (r // CHUNK) * CHUNK
    valid = (rel >= 0) & (rel < HIST_A + CHUNK)
    for h in range(N_HEADS):
        rows = jnp.broadcast_to(gv[h:h + 1, :], (group, width))
        rows = pltpu.roll(rows, width - group + 1, 1, stride=1, stride_axis=0)
        out_ref[h // 2, (h % 2) * group:(h % 2 + 1) * group, :] = jnp.where(valid, rows[:, :band], NEG_INF)


def _rel_bias(table, group):
    band = HIST_A + group
    width = -(-(HIST_A + 2 * group - 1) // LANES) * LANES
    return pl.pallas_call(
        functools.partial(_rel_bias_kernel, group=group, width=width),
        out_shape=jax.ShapeDtypeStruct((N_PAIRS, 2 * group, band), F32),
        name=f"rel_bias_g{group}",
    )(table)


def _alibi_bias(group):
    r = np.arange(group)[:, None]
    j = np.arange(HIST_B + group)[None, :]
    d = np.abs(HIST_B + r - j).astype(np.float32)
    slopes = (2.0 ** (-8.0 * np.arange(1, N_HEADS + 1, dtype=np.float32) / N_HEADS)).astype(np.float32)
    bias = (-slopes[:, None, None] * d[None]).astype(np.float64) * LOG2E
    rel = j - (r // CHUNK) * CHUNK
    valid = (rel >= 0) & (rel < HIST_B + CHUNK)
    bias = np.where(valid[None], bias, np.float32(NEG_INF)).astype(np.float32)
    return jnp.asarray(bias.reshape(N_KV_B, GROUP_B * group, HIST_B + group))


def _inproj_kernel(x_ref, g_ref, w_ref, z_ref, h_ref, *, tn):
    x = x_ref[...]
    ms = jnp.mean(x * x, axis=-1, keepdims=True)
    h_ref[...] = (x * lax.rsqrt(ms + RMS_EPS) * g_ref[...]).astype(h_ref.dtype)
    for n0 in range(0, w_ref.shape[1], tn):
        res = jnp.dot(h_ref[...], w_ref[:, n0:n0 + tn], preferred_element_type=F32).astype(z_ref.dtype)
        for j in range(tn // LANES):
            z_ref[n0 // LANES + j] = res[:, j * LANES:(j + 1) * LANES]


def _inproj(x2d, gain, w_bf16, out_dtype, tm):
    m, d = x2d.shape
    return pl.pallas_call(
        functools.partial(_inproj_kernel, tn=512),
        grid=(m // tm,),
        in_specs=[
            pl.BlockSpec((tm, d), lambda i: (i, 0)),
            _resident((1, d)),
            _resident((d, D_IN)),
        ],
        out_specs=pl.BlockSpec((N_COLS, tm, LANES), lambda i: (0, i, 0)),
        out_shape=jax.ShapeDtypeStruct((N_COLS, m, LANES), out_dtype),
        scratch_shapes=[pltpu.VMEM((tm, d), BF16)],
        compiler_params=pltpu.CompilerParams(
            dimension_semantics=("arbitrary",), vmem_limit_bytes=VMEM_LIMIT_BYTES),
        name=f"inproj_m{m}",
    )(x2d, gain, w_bf16)


def _dup_halves(x):
    lane = lax.broadcasted_iota(jnp.int32, x.shape, 1)
    rolled = pltpu.roll(x, HEAD_DIM, 1)
    return (jnp.where(lane < HEAD_DIM, x, rolled).astype(BF16),
            jnp.where(lane < HEAD_DIM, rolled, x).astype(BF16))


def _split_heads(q):
    lane = lax.broadcasted_iota(jnp.int32, q.shape, 1)
    zero = jnp.zeros_like(q)
    return [jnp.where(lane < HEAD_DIM, q, zero), jnp.where(lane >= HEAD_DIM, q, zero)]


def _divmod_pow2(b, n):
    if isinstance(b, int):
        return divmod(b, n)
    return lax.shift_right_logical(b, n.bit_length() - 1), b & (n - 1)


def _pipeline(n_blocks, stage1, stage2, stage3):
    stage1(0, 0)
    stage1(1, 1)
    stage2(0, 0)

    def body(i, carry):
        b = 2 * i
        stage1(b, 0)
        stage2(b - 1, 1)
        stage3(b - 2, 0)
        stage1(b + 1, 1)
        stage2(b, 0)
        stage3(b - 1, 1)
        return carry

    lax.fori_loop(1, n_blocks // 2, body, 0)
    stage2(n_blocks - 1, 1)
    stage3(n_blocks - 2, 0)
    stage3(n_blocks - 1, 1)


def _mixer_kernel(*refs, tq, group, n_hist_a, n_hist_b, has_cache, n_tiles):
    if has_cache:
        z_ref, x_ref, cak_ref, cav_ref, cbk_ref, cbv_ref = refs[:6]
        refs = refs[6:]
    else:
        z_ref, x_ref = refs[:2]
        refs = refs[2:]
    (bias_a_ref, bias_b_ref, sinks_ref, wout_ref, gfin_ref, y_ref,
     ka_s, va_s, kb_s, vb_s, sa_s, pa_s, ma_s, la_s, sb_s, pb_s, mb_s, lb_s, o3_s, o_s, y_s) = refs
    t = pl.program_id(1)
    band_a = HIST_A + group
    band_b = HIST_B + group

    @pl.when(t == 0)
    def _init_history():
        for p in range(N_PAIRS):
            if has_cache:
                ka_s[p, 0:HIST_A, :] = cak_ref[0, :, p * LANES:(p + 1) * LANES].astype(BF16)
                va_s[p, 0:HIST_A, :] = cav_ref[0, :, p * LANES:(p + 1) * LANES].astype(BF16)
            else:
                ka_s[p, 0:HIST_A, :] = jnp.zeros((HIST_A, LANES), BF16)
                va_s[p, 0:HIST_A, :] = jnp.zeros((HIST_A, LANES), BF16)
        for blk in range(KV_B // LANES):
            for buf, cache in ((kb_s, cbk_ref if has_cache else None), (vb_s, cbv_ref if has_cache else None)):
                if has_cache:
                    lo, hi = _dup_halves(cache[0, :, blk * LANES:(blk + 1) * LANES])
                else:
                    lo = hi = jnp.zeros((HIST_B, LANES), BF16)
                buf[2 * blk, 0:HIST_B, :] = lo
                buf[2 * blk + 1, 0:HIST_B, :] = hi

    for p in range(N_PAIRS):
        ka_s[p, HIST_A:HIST_A + tq, :] = z_ref[KA_C + p].astype(BF16)
        va_s[p, HIST_A:HIST_A + tq, :] = z_ref[VA_C + p].astype(BF16)
    for blk in range(KV_B // LANES):
        for buf, c0 in ((kb_s, KB_C), (vb_s, VB_C)):
            lo, hi = _dup_halves(z_ref[c0 + blk].astype(F32))
            buf[2 * blk, HIST_B:HIST_B + tq, :] = lo
            buf[2 * blk + 1, HIST_B:HIST_B + tq, :] = hi

    lane = lax.broadcasted_iota(jnp.int32, (group, LANES), 1)
    n_groups = tq // group

    def lanes_to(x, width):
        return jnp.tile(x, (1, -(-width // LANES)))[:, :width]

    def stat(x):
        return jnp.broadcast_to(x, (x.shape[0], LANES))

    def attend(masked):
        def rows_of(sg):
            return pl.ds(pl.multiple_of(sg * group, group), group)

        def band_of(sg, band):
            return pl.ds(pl.multiple_of(sg * group, group), band)

        def load_q(c, sg):
            return (z_ref[c, rows_of(sg), :].astype(F32) * (HEAD_DIM ** -0.5 * LOG2E)).astype(BF16)

        def store_gated(c_out, c_gate, sg, o):
            g = z_ref[c_gate, rows_of(sg), :].astype(F32)
            o3_s[c_out, rows_of(sg), :] = (o * (g * jax.nn.sigmoid(g))).astype(BF16)

        def mask_history(s, sg, hist, n_hist):
            if not masked:
                return s
            col = lax.broadcasted_iota(jnp.int32, s.shape, 1)
            return jnp.where(col >= hist - n_hist - (t * tq + sg * group), s, NEG_INF)

        def a_scores(b, slot):
            sg, p = _divmod_pow2(b, N_PAIRS)
            q2 = jnp.concatenate(_split_heads(load_q(QA_C + p, sg)), axis=0)
            k = ka_s[p, band_of(sg, band_a), :]
            s = lax.dot_general(q2, k, _NT, preferred_element_type=F32) + bias_a_ref[p]
            s = mask_history(s, sg, HIST_A, n_hist_a)
            sa_s[slot] = s
            ma_s[slot] = stat(jnp.max(s, axis=-1, keepdims=True))

        def a_softmax(b, slot):
            e = jnp.exp2(sa_s[slot] - lanes_to(ma_s[slot], band_a))
            la_s[slot] = stat(jnp.sum(e, axis=-1, keepdims=True))
            pa_s[slot] = e.astype(BF16)

        def a_values(b, slot):
            sg, p = _divmod_pow2(b, N_PAIRS)
            v = va_s[p, band_of(sg, band_a), :]
            pv = jnp.dot(pa_s[slot], v, preferred_element_type=F32) / la_s[slot]
            store_gated(p, GA_C + p, sg, jnp.where(lane < HEAD_DIM, pv[:group], pv[group:]))

        def sink_rows(g):
            return jnp.concatenate(
                [jnp.full((group, LANES), sinks_ref[GROUP_B * g + i] * LOG2E, F32) for i in range(GROUP_B)],
                axis=0)

        def b_scores(b, slot):
            sg, g = _divmod_pow2(b, N_KV_B)
            q4 = []
            for lb in range(2):
                q4 += _split_heads(load_q(QB_C + 2 * g + lb, sg))
            k = kb_s[g, band_of(sg, band_b), :]
            s = lax.dot_general(jnp.concatenate(q4, axis=0), k, _NT, preferred_element_type=F32) + bias_b_ref[g]
            s = mask_history(s, sg, HIST_B, n_hist_b)
            sb_s[slot] = s
            mb_s[slot] = jnp.maximum(stat(jnp.max(s, axis=-1, keepdims=True)), sink_rows(g))

        def b_softmax(b, slot):
            _, g = _divmod_pow2(b, N_KV_B)
            m = mb_s[slot]
            e = jnp.exp2(sb_s[slot] - lanes_to(m, band_b))
            lb_s[slot] = stat(jnp.sum(e, axis=-1, keepdims=True)) + jnp.exp2(sink_rows(g) - m)
            pb_s[slot] = e.astype(BF16)

        def b_values(b, slot):
            sg, g = _divmod_pow2(b, N_KV_B)
            v = vb_s[g, band_of(sg, band_b), :]
            pv = jnp.dot(pb_s[slot], v, preferred_element_type=F32) / lb_s[slot]
            for lb in range(2):
                top = pv[2 * lb * group:(2 * lb + 1) * group]
                bot = pv[(2 * lb + 1) * group:(2 * lb + 2) * group]
                c = 2 * g + lb
                store_gated(N_PAIRS + c, GB_C + c, sg, jnp.where(lane < HEAD_DIM, top, bot))

        _pipeline(n_groups * N_PAIRS, a_scores, a_softmax, a_values)
        _pipeline(n_groups * N_KV_B, b_scores, b_softmax, b_values)

    if n_hist_a < HIST_A or n_hist_b < HIST_B:
        pl.when(t * tq < HIST_A)(lambda: attend(True))
        pl.when(t * tq >= HIST_A)(lambda: attend(False))
    else:
        attend(False)

    for c in range(2 * N_PAIRS):
        o_s[:, c * LANES:(c + 1) * LANES] = o3_s[c]
    d_model = y_s.shape[1]
    tn = 512
    for n0 in range(0, d_model, tn):
        y_s[:, n0:n0 + tn] = x_ref[0, :, n0:n0 + tn] + jnp.dot(
            o_s[...], wout_ref[:, n0:n0 + tn], preferred_element_type=F32)
    y = y_s[...]
    ms = jnp.mean(y * y, axis=-1, keepdims=True)
    y_ref[0] = y * lax.rsqrt(ms + RMS_EPS) * gfin_ref[...]

    if n_tiles > 1:
        for buf, hist in ((ka_s, HIST_A), (va_s, HIST_A), (kb_s, HIST_B), (vb_s, HIST_B)):
            step = min(tq, hist)
            for d0 in range(0, hist, step):
                buf[:, d0:d0 + step, :] = buf[:, d0 + tq:d0 + tq + step, :]


def _mixer(z3, x, caches, bias_a, bias_b, sinks, wout_bf16, gfin, tq, group):
    nb, s, d = x.shape
    n_tiles = s // tq
    has_cache = caches is not None
    band_a = HIST_A + group
    band_b = HIST_B + group
    in_specs = [pl.BlockSpec((N_COLS, tq, LANES), lambda b, t: (0, b * n_tiles + t, 0)),
                pl.BlockSpec((1, tq, d), lambda b, t: (b, t, 0))]
    args = [z3, x]
    if has_cache:
        for c in caches:
            in_specs.append(pl.BlockSpec((1,) + c.shape[1:], lambda b, t: (b, 0, 0)))
            args.append(c)
    in_specs += [
        _resident(bias_a.shape),
        _resident(bias_b.shape),
        pl.BlockSpec(memory_space=pltpu.SMEM),
        _resident(wout_bf16.shape),
        _resident(gfin.shape),
    ]
    args += [bias_a, bias_b, sinks, wout_bf16, gfin]
    kern = functools.partial(
        _mixer_kernel, tq=tq, group=group,
        n_hist_a=HIST_A if has_cache else 0, n_hist_b=HIST_B if has_cache else 0,
        has_cache=has_cache, n_tiles=n_tiles)
    return pl.pallas_call(
        kern,
        grid=(nb, n_tiles),
        in_specs=in_specs,
        out_specs=pl.BlockSpec((1, tq, d), lambda b, t: (b, t, 0)),
        out_shape=jax.ShapeDtypeStruct((nb, s, d), F32),
        scratch_shapes=[
            pltpu.VMEM((N_PAIRS, HIST_A + tq, LANES), BF16),
            pltpu.VMEM((N_PAIRS, HIST_A + tq, LANES), BF16),
            pltpu.VMEM((N_KV_B, HIST_B + tq, LANES), BF16),
            pltpu.VMEM((N_KV_B, HIST_B + tq, LANES), BF16),
            pltpu.VMEM((2, 2 * group, band_a), F32),
            pltpu.VMEM((2, 2 * group, band_a), BF16),
            pltpu.VMEM((2, 2 * group, LANES), F32),
            pltpu.VMEM((2, 2 * group, LANES), F32),
            pltpu.VMEM((2, GROUP_B * group, band_b), F32),
            pltpu.VMEM((2, GROUP_B * group, band_b), BF16),
            pltpu.VMEM((2, GROUP_B * group, LANES), F32),
            pltpu.VMEM((2, GROUP_B * group, LANES), F32),
            pltpu.VMEM((2 * N_PAIRS, tq, LANES), BF16),
            pltpu.VMEM((tq, D_A + D_B), BF16),
            pltpu.VMEM((tq, d), F32),
        ],
        compiler_params=pltpu.CompilerParams(
            dimension_semantics=("arbitrary", "arbitrary"), vmem_limit_bytes=VMEM_LIMIT_BYTES),
        name=f"mixer_tq{tq}",
    )(*args)


def kernel(x_prompt, x_sample, cache_a_k, cache_a_v, cache_b_k, cache_b_v, norm_in, w_in, rel_bias_a,
           sinks_b, w_out, norm_final):
    b_p, seq, d = x_prompt.shape
    b_s, n_new, _ = x_sample.shape
    assert norm_in.shape[0] == 1, "one layer: the final norm is fused into the mixer call"
    assert n_new == CHUNK and seq % 256 == 0 and seq >= HIST_A
    assert cache_a_k.shape[2] == HIST_A and cache_b_k.shape[2] == HIST_B

    w_in_b = w_in[0].astype(BF16)
    w_out_b = w_out[0].astype(BF16)
    g_in = norm_in[0].reshape(1, d)
    g_fin = norm_final.reshape(1, d)
    sinks = sinks_b[0]

    z_p = _inproj(x_prompt.reshape(b_p * seq, d), g_in, w_in_b, BF16, tm=512)
    y_p = _mixer(z_p, x_prompt, None, _rel_bias(rel_bias_a[0], 128), _alibi_bias(128), sinks,
                 w_out_b, g_fin, tq=256, group=128)

    z_s = _inproj(x_sample.reshape(b_s * n_new, d), g_in, w_in_b, F32, tm=b_s * n_new)
    caches = (cache_a_k[0].reshape(b_s, HIST_A, D_A), cache_a_v[0].reshape(b_s, HIST_A, D_A),
              cache_b_k[0].reshape(b_s, HIST_B, KV_B), cache_b_v[0].reshape(b_s, HIST_B, KV_B))
    y_s = _mixer(z_s, x_sample, caches, _rel_bias(rel_bias_a[0], CHUNK), _alibi_bias(CHUNK), sinks,
                 w_out_b, g_fin, tq=CHUNK, group=CHUNK)

    def heads(z3, c0, n_heads, nb, rows):
        blk = z3[c0:c0 + n_heads // 2].reshape(n_heads // 2, nb, -1, LANES)[:, :, rows, :]
        return jnp.moveaxis(blk, 0, 2).astype(F32).reshape(nb, -1, n_heads, HEAD_DIM)

    def tail(c0, n_heads, keep):
        return heads(z_p, c0, n_heads, b_p, slice(seq - keep, seq))[None]

    def rolled(cache, c0, n_heads):
        new = heads(z_s, c0, n_heads, b_s, slice(None))
        return jnp.concatenate([cache[0], new], axis=1)[None, :, n_new:]

    return (y_p, y_s,
            tail(KA_C, N_HEADS, HIST_A), tail(VA_C, N_HEADS, HIST_A),
            tail(KB_C, N_KV_B, HIST_B), tail(VB_C, N_KV_B, HIST_B),
            rolled(cache_a_k, KA_C, N_HEADS), rolled(cache_a_v, VA_C, N_HEADS),
            rolled(cache_b_k, KB_C, N_KV_B), rolled(cache_b_v, VB_C, N_KV_B))
```

```python
import functools

import jax
import jax.numpy as jnp
import numpy as np
from jax import lax
from jax.experimental import pallas as pl
from jax.experimental.pallas import tpu as pltpu

CHUNK = 64
HEAD_DIM = 64
N_HEADS = 16
N_KV_B = 4
GROUP_B = N_HEADS // N_KV_B
D_A = N_HEADS * HEAD_DIM
D_B = N_HEADS * HEAD_DIM
KV_B = N_KV_B * HEAD_DIM
HIST_A = 8 * CHUNK
HIST_B = 2 * CHUNK
REL_CLIP = 256
N_REL = REL_CLIP + CHUNK
RMS_EPS = 1e-6
NEG_INF = -1e30
LOG2E = 1.4426950408889634

LANES = 128
N_PAIRS = D_A // LANES
QA_C, KA_C, VA_C, GA_C = 0, N_PAIRS, 2 * N_PAIRS, 3 * N_PAIRS
QB_C = 4 * N_PAIRS
KB_C = QB_C + N_PAIRS
VB_C = KB_C + KV_B // LANES
GB_C = VB_C + KV_B // LANES
N_COLS = GB_C + N_PAIRS
D_IN = N_COLS * LANES

VMEM_LIMIT_BYTES = 56 * 1024 * 1024

BF16 = jnp.bfloat16
F32 = jnp.float32
_NT = (((1,), (1,)), ((), ()))


def _resident(shape):
    return pl.BlockSpec(shape, lambda *_: (0,) * len(shape), pipeline_mode=pl.Buffered(1))


def _rel_bias_kernel(tab_ref, out_ref, *, group, width):
    band = out_ref.shape[2]
    n = lax.broadcasted_iota(jnp.int32, (N_REL, width), 0)
    u = lax.broadcasted_iota(jnp.int32, (N_REL, width), 1)
    idx = jnp.clip(HIST_A + group - 1 - u, -(CHUNK - 1), REL_CLIP) + (CHUNK - 1)
    onehot = (n == idx).astype(F32)
    gv = jnp.dot(tab_ref[...], onehot, precision=lax.Precision.HIGHEST, preferred_element_type=F32) * LOG2E
    r = lax.broadcasted_iota(jnp.int32, (group, band), 0)
    j = lax.broadcasted_iota(jnp.int32, (group, band), 1)
    rel = j - (r // CHUNK) * CHUNK
    valid = (rel >= 0) & (rel < HIST_A + CHUNK)
    for h in range(N_HEADS):
        rows = jnp.broadcast_to(gv[h:h + 1, :], (group, width))
        rows = pltpu.roll(rows, width - group + 1, 1, stride=1, stride_axis=0)
        out_ref[h // 2, (h % 2) * group:(h % 2 + 1) * group, :] = jnp.where(valid, rows[:, :band], NEG_INF)


def _rel_bias(table, group):
    band = HIST_A + group
    width = -(-(HIST_A + 2 * group - 1) // LANES) * LANES
    return pl.pallas_call(
        functools.partial(_rel_bias_kernel, group=group, width=width),
        out_shape=jax.ShapeDtypeStruct((N_PAIRS, 2 * group, band), F32),
        name=f"rel_bias_g{group}",
    )(table)


def _alibi_bias(group):
    r = np.arange(group)[:, None]
    j = np.arange(HIST_B + group)[None, :]
    d = np.abs(HIST_B + r - j).astype(np.float32)
    slopes = (2.0 ** (-8.0 * np.arange(1, N_HEADS + 1, dtype=np.float32) / N_HEADS)).astype(np.float32)
    bias = (-slopes[:, None, None] * d[None]).astype(np.float64) * LOG2E
    rel = j - (r // CHUNK) * CHUNK
    valid = (rel >= 0) & (rel < HIST_B + CHUNK)
    bias = np.where(valid[None], bias, np.float32(NEG_INF)).astype(np.float32)
    return jnp.asarray(bias.reshape(N_KV_B, GROUP_B * group, HIST_B + group))


def _inproj_kernel(x_ref, g_ref, w_ref, z_ref, h_ref, *, tn):
    x = x_ref[...]
    ms = jnp.mean(x * x, axis=-1, keepdims=True)
    h_ref[...] = (x * lax.rsqrt(ms + RMS_EPS) * g_ref[...]).astype(h_ref.dtype)
    for n0 in range(0, w_ref.shape[1], tn):
        res = jnp.dot(h_ref[...], w_ref[:, n0:n0 + tn], preferred_element_type=F32)
        c0 = n0 // LANES
        if c0 in Q_CHUNKS:
            res = res * (HEAD_DIM ** -0.5 * LOG2E)
        res = res.astype(z_ref.dtype)
        for j in range(tn // LANES):
            z_ref[c0 + j] = res[:, j * LANES:(j + 1) * LANES]


INPROJ_TN = 4 * LANES
Q_CHUNKS = tuple(c for c0 in (QA_C, QB_C) for c in range(c0, c0 + N_PAIRS, INPROJ_TN // LANES))


def _inproj(x2d, gain, w_bf16, out_dtype, tm):
    m, d = x2d.shape
    return pl.pallas_call(
        functools.partial(_inproj_kernel, tn=INPROJ_TN),
        grid=(m // tm,),
        in_specs=[
            pl.BlockSpec((tm, d), lambda i: (i, 0)),
            _resident((1, d)),
            _resident((d, D_IN)),
        ],
        out_specs=pl.BlockSpec((N_COLS, tm, LANES), lambda i: (0, i, 0)),
        out_shape=jax.ShapeDtypeStruct((N_COLS, m, LANES), out_dtype),
        scratch_shapes=[pltpu.VMEM((tm, d), BF16)],
        compiler_params=pltpu.CompilerParams(
            dimension_semantics=("arbitrary",), vmem_limit_bytes=VMEM_LIMIT_BYTES),
        name=f"inproj_m{m}",
    )(x2d, gain, w_bf16)


def _dup_halves(x):
    lane = lax.broadcasted_iota(jnp.int32, x.shape, 1)
    rolled = pltpu.roll(x, HEAD_DIM, 1)
    return (jnp.where(lane < HEAD_DIM, x, rolled).astype(BF16),
            jnp.where(lane < HEAD_DIM, rolled, x).astype(BF16))


def _split_heads(q):
    lane = lax.broadcasted_iota(jnp.int32, q.shape, 1)
    zero = jnp.zeros_like(q)
    return [jnp.where(lane < HEAD_DIM, q, zero), jnp.where(lane >= HEAD_DIM, q, zero)]


def _divmod_pow2(b, n):
    if isinstance(b, int):
        return divmod(b, n)
    return lax.shift_right_logical(b, n.bit_length() - 1), b & (n - 1)


def _pipeline(n_blocks, stage1, stage2, stage3, unroll):
    assert unroll % 2 == 0 and n_blocks % 2 == 0

    def advance(b0, k):
        stage1(b0 + k, k % 2)
        stage2(b0 + k - 1, (k - 1) % 2)
        stage3(b0 + k - 2, k % 2)

    stage1(0, 0)
    stage1(1, 1)
    stage2(0, 0)
    n_iter = (n_blocks - 2) // unroll

    def body(i, carry):
        for k in range(unroll):
            advance(2 + unroll * i, k)
        return carry

    lax.fori_loop(0, n_iter, body, 0)
    for b in range(2 + unroll * n_iter, n_blocks):
        advance(b - b % 2, b % 2)
    stage2(n_blocks - 1, 1)
    stage3(n_blocks - 2, 0)
    stage3(n_blocks - 1, 1)


def _mixer_kernel(*refs, tq, group, n_hist_a, n_hist_b, has_cache, n_tiles, keep_a, keep_b):
    if has_cache:
        z_ref, x_ref, cak_ref, cav_ref, cbk_ref, cbv_ref = refs[:6]
        refs = refs[6:]
    else:
        z_ref, x_ref = refs[:2]
        refs = refs[2:]
    (bias_a_ref, bias_b_ref, sinks_ref, wout_ref, gfin_ref,
     y_ref, tak_ref, tav_ref, tbk_ref, tbv_ref,
     ka_s, va_s, kb_s, vb_s, sa_s, pa_s, ma_s, la_s, sb_s, pb_s, mb_s, lb_s, o3_s, o_s, y_s) = refs
    t = pl.program_id(1)
    band_a = HIST_A + group
    band_b = HIST_B + group
    n_groups = tq // group

    def init_history():
        for p in range(N_PAIRS):
            if has_cache:
                ka_s[p, 0:HIST_A, :] = cak_ref[0, :, p * LANES:(p + 1) * LANES].astype(BF16)
                va_s[p, 0:HIST_A, :] = cav_ref[0, :, p * LANES:(p + 1) * LANES].astype(BF16)
            else:
                ka_s[p, 0:HIST_A, :] = jnp.zeros((HIST_A, LANES), BF16)
                va_s[p, 0:HIST_A, :] = jnp.zeros((HIST_A, LANES), BF16)
        for blk in range(KV_B // LANES):
            for buf, cache in ((kb_s, cbk_ref if has_cache else None), (vb_s, cbv_ref if has_cache else None)):
                if has_cache:
                    lo, hi = _dup_halves(cache[0, :, blk * LANES:(blk + 1) * LANES])
                else:
                    lo = hi = jnp.zeros((HIST_B, LANES), BF16)
                buf[2 * blk, 0:HIST_B, :] = lo
                buf[2 * blk + 1, 0:HIST_B, :] = hi

    def write_tail(ref, c0, n_cblk, keep):
        def copy(rows):
            for p in range(n_cblk):
                ref[0, :, p * LANES:(p + 1) * LANES] = z_ref[c0 + p, rows, :].astype(F32)
        if keep >= tq:
            pl.when(t >= n_tiles - keep // tq)(lambda: copy(slice(None)))
        else:
            pl.when(t == n_tiles - 1)(lambda: copy(slice(tq - keep, tq)))

    lane = lax.broadcasted_iota(jnp.int32, (group, LANES), 1)

    def lanes_to(x, width):
        return jnp.tile(x, (1, -(-width // LANES)))[:, :width]

    def stat(x):
        return jnp.broadcast_to(x, (x.shape[0], LANES))

    def attend(masked):
        def rows_of(sg):
            return pl.ds(pl.multiple_of(sg * group, group), group)

        def band_of(sg, band):
            return pl.ds(pl.multiple_of(sg * group, group), band)

        def load_q(c, sg):
            return z_ref[c, rows_of(sg), :].astype(BF16)

        def store_gated(c_out, c_gate, sg, o):
            g = z_ref[c_gate, rows_of(sg), :].astype(F32)
            o3_s[c_out, rows_of(sg), :] = (o * (g * jax.nn.sigmoid(g))).astype(BF16)

        def mask_history(s, sg, hist, n_hist):
            if not masked:
                return s
            col = lax.broadcasted_iota(jnp.int32, s.shape, 1)
            return jnp.where(col >= hist - n_hist - (t * tq + sg * group), s, NEG_INF)

        def a_scores(b, slot):
            sg, p = _divmod_pow2(b, N_PAIRS)
            q2 = jnp.concatenate(_split_heads(load_q(QA_C + p, sg)), axis=0)
            k = ka_s[p, band_of(sg, band_a), :]
            s = lax.dot_general(q2, k, _NT, preferred_element_type=F32) + bias_a_ref[p]
            s = mask_history(s, sg, HIST_A, n_hist_a)
            sa_s[slot] = s
            ma_s[slot] = stat(jnp.max(s, axis=-1, keepdims=True))

        def a_softmax(b, slot):
            e = jnp.exp2(sa_s[slot] - lanes_to(ma_s[slot], band_a))
            la_s[slot] = stat(jnp.sum(e, axis=-1, keepdims=True))
            pa_s[slot] = e.astype(BF16)

        def a_values(b, slot):
            sg, p = _divmod_pow2(b, N_PAIRS)
            v = va_s[p, band_of(sg, band_a), :]
            pv = jnp.dot(pa_s[slot], v, preferred_element_type=F32) / la_s[slot]
            store_gated(p, GA_C + p, sg, jnp.where(lane < HEAD_DIM, pv[:group], pv[group:]))

        def sink_rows(g):
            return jnp.concatenate(
                [jnp.full((group, LANES), sinks_ref[GROUP_B * g + i] * LOG2E, F32) for i in range(GROUP_B)],
                axis=0)

        def b_scores(b, slot):
            sg, g = _divmod_pow2(b, N_KV_B)
            q4 = []
            for lb in range(2):
                q4 += _split_heads(load_q(QB_C + 2 * g + lb, sg))
            k = kb_s[g, band_of(sg, band_b), :]
            s = lax.dot_general(jnp.concatenate(q4, axis=0), k, _NT, preferred_element_type=F32) + bias_b_ref[g]
            s = mask_history(s, sg, HIST_B, n_hist_b)
            sb_s[slot] = s
            mb_s[slot] = jnp.maximum(stat(jnp.max(s, axis=-1, keepdims=True)), sink_rows(g))

        def b_softmax(b, slot):
            _, g = _divmod_pow2(b, N_KV_B)
            m = mb_s[slot]
            e = jnp.exp2(sb_s[slot] - lanes_to(m, band_b))
            lb_s[slot] = stat(jnp.sum(e, axis=-1, keepdims=True)) + jnp.exp2(sink_rows(g) - m)
            pb_s[slot] = e.astype(BF16)

        def b_values(b, slot):
            sg, g = _divmod_pow2(b, N_KV_B)
            v = vb_s[g, band_of(sg, band_b), :]
            pv = jnp.dot(pb_s[slot], v, preferred_element_type=F32) / lb_s[slot]
            for lb in range(2):
                top = pv[2 * lb * group:(2 * lb + 1) * group]
                bot = pv[(2 * lb + 1) * group:(2 * lb + 2) * group]
                c = 2 * g + lb
                store_gated(N_PAIRS + c, GB_C + c, sg, jnp.where(lane < HEAD_DIM, top, bot))

        _pipeline(n_groups * N_PAIRS, a_scores, a_softmax, a_values, unroll=4)
        _pipeline(n_groups * N_KV_B, b_scores, b_softmax, b_values, unroll=2)

    pl.when(t == 0)(init_history)
    for p in range(N_PAIRS):
        ka_s[p, HIST_A:HIST_A + tq, :] = z_ref[KA_C + p].astype(BF16)
        va_s[p, HIST_A:HIST_A + tq, :] = z_ref[VA_C + p].astype(BF16)
    for blk in range(KV_B // LANES):
        for buf, c0 in ((kb_s, KB_C), (vb_s, VB_C)):
            lo, hi = _dup_halves(z_ref[c0 + blk].astype(F32))
            buf[2 * blk, HIST_B:HIST_B + tq, :] = lo
            buf[2 * blk + 1, HIST_B:HIST_B + tq, :] = hi
    write_tail(tak_ref, KA_C, N_PAIRS, keep_a)
    write_tail(tav_ref, VA_C, N_PAIRS, keep_a)
    write_tail(tbk_ref, KB_C, KV_B // LANES, keep_b)
    write_tail(tbv_ref, VB_C, KV_B // LANES, keep_b)

    if n_hist_a < HIST_A or n_hist_b < HIST_B:
        pl.when(t * tq < HIST_A)(lambda: attend(True))
        pl.when(t * tq >= HIST_A)(lambda: attend(False))
    else:
        attend(False)

    for c in range(2 * N_PAIRS):
        o_s[:, c * LANES:(c + 1) * LANES] = o3_s[c]
    d_model = y_s.shape[1]
    tn = 512
    for n0 in range(0, d_model, tn):
        y_s[:, n0:n0 + tn] = x_ref[0, :, n0:n0 + tn] + jnp.dot(
            o_s[...], wout_ref[:, n0:n0 + tn], preferred_element_type=F32)
    y = y_s[...]
    ms = jnp.mean(y * y, axis=-1, keepdims=True)
    y_ref[0] = y * lax.rsqrt(ms + RMS_EPS) * gfin_ref[...]

    if n_tiles > 1:
        for buf, hist in ((ka_s, HIST_A), (va_s, HIST_A), (kb_s, HIST_B), (vb_s, HIST_B)):
            step = min(tq, hist)
            for d0 in range(0, hist, step):
                buf[:, d0:d0 + step, :] = buf[:, d0 + tq:d0 + tq + step, :]


def _mixer(z3, x, caches, bias_a, bias_b, sinks, wout_bf16, gfin, tq, group, keep_a, keep_b):
    nb, s, d = x.shape
    n_tiles = s // tq
    has_cache = caches is not None
    band_a = HIST_A + group
    band_b = HIST_B + group

    def tail_spec(keep, width):
        if keep >= tq:
            nblk = keep // tq
            return pl.BlockSpec((1, tq, width), lambda b, t: (b, jnp.clip(t - (n_tiles - nblk), 0, nblk - 1), 0))
        return pl.BlockSpec((1, keep, width), lambda b, t: (b, 0, 0))

    in_specs = [pl.BlockSpec((N_COLS, tq, LANES), lambda b, t: (0, b * n_tiles + t, 0)),
                pl.BlockSpec((1, tq, d), lambda b, t: (b, t, 0))]
    args = [z3, x]
    if has_cache:
        for c in caches:
            in_specs.append(pl.BlockSpec((1,) + c.shape[1:], lambda b, t: (b, 0, 0)))
            args.append(c)
    in_specs += [
        _resident(bias_a.shape),
        _resident(bias_b.shape),
        pl.BlockSpec(memory_space=pltpu.SMEM),
        _resident(wout_bf16.shape),
        _resident(gfin.shape),
    ]
    args += [bias_a, bias_b, sinks, wout_bf16, gfin]
    kern = functools.partial(
        _mixer_kernel, tq=tq, group=group,
        n_hist_a=HIST_A if has_cache else 0, n_hist_b=HIST_B if has_cache else 0,
        has_cache=has_cache, n_tiles=n_tiles, keep_a=keep_a, keep_b=keep_b)
    return pl.pallas_call(
        kern,
        grid=(nb, n_tiles),
        in_specs=in_specs,
        out_specs=[
            pl.BlockSpec((1, tq, d), lambda b, t: (b, t, 0)),
            tail_spec(keep_a, D_A), tail_spec(keep_a, D_A),
            tail_spec(keep_b, KV_B), tail_spec(keep_b, KV_B),
        ],
        out_shape=[
            jax.ShapeDtypeStruct((nb, s, d), F32),
            jax.ShapeDtypeStruct((nb, keep_a, D_A), F32), jax.ShapeDtypeStruct((nb, keep_a, D_A), F32),
            jax.ShapeDtypeStruct((nb, keep_b, KV_B), F32), jax.ShapeDtypeStruct((nb, keep_b, KV_B), F32),
        ],
        scratch_shapes=[
            pltpu.VMEM((N_PAIRS, HIST_A + tq, LANES), BF16),
            pltpu.VMEM((N_PAIRS, HIST_A + tq, LANES), BF16),
            pltpu.VMEM((N_KV_B, HIST_B + tq, LANES), BF16),
            pltpu.VMEM((N_KV_B, HIST_B + tq, LANES), BF16),
            pltpu.VMEM((2, 2 * group, band_a), F32),
            pltpu.VMEM((2, 2 * group, band_a), BF16),
            pltpu.VMEM((2, 2 * group, LANES), F32),
            pltpu.VMEM((2, 2 * group, LANES), F32),
            pltpu.VMEM((2, GROUP_B * group, band_b), F32),
            pltpu.VMEM((2, GROUP_B * group, band_b), BF16),
            pltpu.VMEM((2, GROUP_B * group, LANES), F32),
            pltpu.VMEM((2, GROUP_B * group, LANES), F32),
            pltpu.VMEM((2 * N_PAIRS, tq, LANES), BF16),
            pltpu.VMEM((tq, D_A + D_B), BF16),
            pltpu.VMEM((tq, d), F32),
        ],
        compiler_params=pltpu.CompilerParams(
            dimension_semantics=("arbitrary", "arbitrary"), vmem_limit_bytes=VMEM_LIMIT_BYTES),
        name=f"mixer_tq{tq}",
    )(*args)


def kernel(x_prompt, x_sample, cache_a_k, cache_a_v, cache_b_k, cache_b_v, norm_in, w_in, rel_bias_a,
           sinks_b, w_out, norm_final):
    b_p, seq, d = x_prompt.shape
    b_s, n_new, _ = x_sample.shape
    assert norm_in.shape[0] == 1, "one layer: the final norm is fused into the mixer call"
    assert n_new == CHUNK and seq % 256 == 0 and seq >= HIST_A
    assert cache_a_k.shape[2] == HIST_A and cache_b_k.shape[2] == HIST_B

    w_in_b = w_in[0].astype(BF16)
    w_out_b = w_out[0].astype(BF16)
    g_in = norm_in[0].reshape(1, d)
    g_fin = norm_final.reshape(1, d)
    sinks = sinks_b[0]

    z_p = _inproj(x_prompt.reshape(b_p * seq, d), g_in, w_in_b, BF16, tm=512)
    y_p, ak_p, av_p, bk_p, bv_p = _mixer(
        z_p, x_prompt, None, _rel_bias(rel_bias_a[0], 128), _alibi_bias(128), sinks, w_out_b, g_fin,
        tq=256, group=128, keep_a=HIST_A, keep_b=HIST_B)

    z_s = _inproj(x_sample.reshape(b_s * n_new, d), g_in, w_in_b, F32, tm=b_s * n_new)
    caches = (cache_a_k[0].reshape(b_s, HIST_A, D_A), cache_a_v[0].reshape(b_s, HIST_A, D_A),
              cache_b_k[0].reshape(b_s, HIST_B, KV_B), cache_b_v[0].reshape(b_s, HIST_B, KV_B))
    y_s, ak_s, av_s, bk_s, bv_s = _mixer(
        z_s, x_sample, caches, _rel_bias(rel_bias_a[0], CHUNK), _alibi_bias(CHUNK), sinks, w_out_b, g_fin,
        tq=CHUNK, group=CHUNK, keep_a=n_new, keep_b=n_new)

    def heads(rows, n_heads):
        return rows.reshape(1, rows.shape[0], rows.shape[1], n_heads, HEAD_DIM)

    def rolled(cache, new, n_heads):
        return jnp.concatenate([cache, heads(new, n_heads)], axis=2)[:, :, n_new:]

    return (y_p, y_s,
            heads(ak_p, N_HEADS), heads(av_p, N_HEADS), heads(bk_p, N_KV_B), heads(bv_p, N_KV_B),
            rolled(cache_a_k, ak_s, N_HEADS), rolled(cache_a_v, av_s, N_HEADS),
            rolled(cache_b_k, bk_s, N_KV_B), rolled(cache_b_v, bv_s, N_KV_B))
```

```python
import functools

import jax
import jax.numpy as jnp
import numpy as np
from jax import lax
from jax.experimental import pallas as pl
from jax.experimental.pallas import tpu as pltpu

CHUNK = 64
HEAD_DIM = 64
N_HEADS = 16
N_KV_B = 4
GROUP_B = N_HEADS // N_KV_B
D_A = N_HEADS * HEAD_DIM
D_B = N_HEADS * HEAD_DIM
KV_B = N_KV_B * HEAD_DIM
HIST_A = 8 * CHUNK
HIST_B = 2 * CHUNK
REL_CLIP = 256
N_REL = REL_CLIP + CHUNK
RMS_EPS = 1e-6
NEG_INF = -1e30
LOG2E = 1.4426950408889634

LANES = 128
N_PAIRS = D_A // LANES
QA_C, KA_C, VA_C, GA_C = 0, N_PAIRS, 2 * N_PAIRS, 3 * N_PAIRS
QB_C = 4 * N_PAIRS
KB_C = QB_C + N_PAIRS
VB_C = KB_C + KV_B // LANES
GB_C = VB_C + KV_B // LANES
N_COLS = GB_C + N_PAIRS
D_IN = N_COLS * LANES

VMEM_LIMIT_BYTES = 56 * 1024 * 1024

BF16 = jnp.bfloat16
F32 = jnp.float32
_NT = (((1,), (1,)), ((), ()))


def _resident(shape):
    return pl.BlockSpec(shape, lambda *_: (0,) * len(shape), pipeline_mode=pl.Buffered(1))


def _rel_bias_kernel(tab_ref, out_ref, *, group, width):
    band = out_ref.shape[2]
    n = lax.broadcasted_iota(jnp.int32, (N_REL, width), 0)
    u = lax.broadcasted_iota(jnp.int32, (N_REL, width), 1)
    idx = jnp.clip(HIST_A + group - 1 - u, -(CHUNK - 1), REL_CLIP) + (CHUNK - 1)
    onehot = (n == idx).astype(F32)
    gv = jnp.dot(tab_ref[...], onehot, precision=lax.Precision.HIGHEST, preferred_element_type=F32) * LOG2E
    r = lax.broadcasted_iota(jnp.int32, (group, band), 0)
    j = lax.broadcasted_iota(jnp.int32, (group, band), 1)
    rel = j - (r // CHUNK) * CHUNK
    valid = (rel >= 0) & (rel < HIST_A + CHUNK)
    for h in range(N_HEADS):
        rows = jnp.broadcast_to(gv[h:h + 1, :], (group, width))
        rows = pltpu.roll(rows, width - group + 1, 1, stride=1, stride_axis=0)
        out_ref[h // 2, (h % 2) * group:(h % 2 + 1) * group, :] = jnp.where(valid, rows[:, :band], NEG_INF)


def _rel_bias(table, group):
    band = HIST_A + group
    width = -(-(HIST_A + 2 * group - 1) // LANES) * LANES
    return pl.pallas_call(
        functools.partial(_rel_bias_kernel, group=group, width=width),
        out_shape=jax.ShapeDtypeStruct((N_PAIRS, 2 * group, band), F32),
        name=f"rel_bias_g{group}",
    )(table)


def _alibi_bias(group):
    r = np.arange(group)[:, None]
    j = np.arange(HIST_B + group)[None, :]
    d = np.abs(HIST_B + r - j).astype(np.float32)
    slopes = (2.0 ** (-8.0 * np.arange(1, N_HEADS + 1, dtype=np.float32) / N_HEADS)).astype(np.float32)
    bias = (-slopes[:, None, None] * d[None]).astype(np.float64) * LOG2E
    rel = j - (r // CHUNK) * CHUNK
    valid = (rel >= 0) & (rel < HIST_B + CHUNK)
    bias = np.where(valid[None], bias, np.float32(NEG_INF)).astype(np.float32)
    return jnp.asarray(bias.reshape(N_KV_B, GROUP_B * group, HIST_B + group))


def _inproj_kernel(x_ref, g_ref, w_ref, z_ref, h_ref, *, tn):
    x = x_ref[...]
    ms = jnp.mean(x * x, axis=-1, keepdims=True)
    h_ref[...] = (x * lax.rsqrt(ms + RMS_EPS) * g_ref[...]).astype(h_ref.dtype)
    for n0 in range(0, w_ref.shape[1], tn):
        res = jnp.dot(h_ref[...], w_ref[:, n0:n0 + tn], preferred_element_type=F32)
        c0 = n0 // LANES
        if c0 in Q_CHUNKS:
            res = res * (HEAD_DIM ** -0.5 * LOG2E)
        res = res.astype(z_ref.dtype)
        for j in range(tn // LANES):
            z_ref[c0 + j] = res[:, j * LANES:(j + 1) * LANES]


INPROJ_TN = 4 * LANES
Q_CHUNKS = tuple(c for c0 in (QA_C, QB_C) for c in range(c0, c0 + N_PAIRS, INPROJ_TN // LANES))


def _inproj(x2d, gain, w_bf16, out_dtype, tm):
    m, d = x2d.shape
    return pl.pallas_call(
        functools.partial(_inproj_kernel, tn=INPROJ_TN),
        grid=(m // tm,),
        in_specs=[
            pl.BlockSpec((tm, d), lambda i: (i, 0)),
            _resident((1, d)),
            _resident((d, D_IN)),
        ],
        out_specs=pl.BlockSpec((N_COLS, tm, LANES), lambda i: (0, i, 0)),
        out_shape=jax.ShapeDtypeStruct((N_COLS, m, LANES), out_dtype),
        scratch_shapes=[pltpu.VMEM((tm, d), BF16)],
        compiler_params=pltpu.CompilerParams(
            dimension_semantics=("arbitrary",), vmem_limit_bytes=VMEM_LIMIT_BYTES),
        name=f"inproj_m{m}",
    )(x2d, gain, w_bf16)


def _dup_halves(x):
    lane = lax.broadcasted_iota(jnp.int32, x.shape, 1)
    rolled = pltpu.roll(x, HEAD_DIM, 1)
    return (jnp.where(lane < HEAD_DIM, x, rolled).astype(BF16),
            jnp.where(lane < HEAD_DIM, rolled, x).astype(BF16))


def _split_heads(q):
    lane = lax.broadcasted_iota(jnp.int32, q.shape, 1)
    zero = jnp.zeros_like(q)
    return [jnp.where(lane < HEAD_DIM, q, zero), jnp.where(lane >= HEAD_DIM, q, zero)]


def _divmod_pow2(b, n):
    if isinstance(b, int):
        return divmod(b, n)
    return lax.shift_right_logical(b, n.bit_length() - 1), b & (n - 1)


def _pipelines(*pipes):
    n_iter = (pipes[0][0] - 2) // pipes[0][2]
    for n_blocks, _, unroll in pipes:
        assert unroll % 2 == 0 and n_blocks % 2 == 0 and (n_blocks - 2) // unroll == n_iter

    def advance(stages, b0, k):
        stages[0](b0 + k, k % 2)
        stages[1](b0 + k - 1, (k - 1) % 2)
        stages[2](b0 + k - 2, k % 2)

    for _, stages, _ in pipes:
        stages[0](0, 0)
        stages[0](1, 1)
        stages[1](0, 0)

    def body(i, carry):
        for _, stages, unroll in pipes:
            for k in range(unroll):
                advance(stages, 2 + unroll * i, k)
        return carry

    lax.fori_loop(0, n_iter, body, 0)
    for n_blocks, stages, unroll in pipes:
        for b in range(2 + unroll * n_iter, n_blocks):
            advance(stages, b - b % 2, b % 2)
        stages[1](n_blocks - 1, 1)
        stages[2](n_blocks - 2, 0)
        stages[2](n_blocks - 1, 1)


def _mixer_kernel(*refs, tq, group, n_hist_a, n_hist_b, has_cache, n_tiles, keep_a, keep_b):
    if has_cache:
        z_ref, x_ref, cak_ref, cav_ref, cbk_ref, cbv_ref = refs[:6]
        refs = refs[6:]
    else:
        z_ref, x_ref = refs[:2]
        refs = refs[2:]
    (bias_a_ref, bias_b_ref, sinks_ref, wout_ref, gfin_ref,
     y_ref, tak_ref, tav_ref, tbk_ref, tbv_ref,
     ka_s, va_s, kb_s, vb_s, sa_s, pa_s, ma_s, la_s, sb_s, pb_s, mb_s, lb_s, o3_s, o_s, y_s) = refs
    t = pl.program_id(1)
    band_a = HIST_A + group
    band_b = HIST_B + group
    n_groups = tq // group

    def init_history():
        for p in range(N_PAIRS):
            if has_cache:
                ka_s[p, 0:HIST_A, :] = cak_ref[0, :, p * LANES:(p + 1) * LANES].astype(BF16)
                va_s[p, 0:HIST_A, :] = cav_ref[0, :, p * LANES:(p + 1) * LANES].astype(BF16)
            else:
                ka_s[p, 0:HIST_A, :] = jnp.zeros((HIST_A, LANES), BF16)
                va_s[p, 0:HIST_A, :] = jnp.zeros((HIST_A, LANES), BF16)
        for blk in range(KV_B // LANES):
            for buf, cache in ((kb_s, cbk_ref if has_cache else None), (vb_s, cbv_ref if has_cache else None)):
                if has_cache:
                    lo, hi = _dup_halves(cache[0, :, blk * LANES:(blk + 1) * LANES])
                else:
                    lo = hi = jnp.zeros((HIST_B, LANES), BF16)
                buf[2 * blk, 0:HIST_B, :] = lo
                buf[2 * blk + 1, 0:HIST_B, :] = hi

    def write_tail(ref, c0, n_cblk, keep):
        def copy(rows):
            for p in range(n_cblk):
                ref[0, :, p * LANES:(p + 1) * LANES] = z_ref[c0 + p, rows, :].astype(F32)
        if keep >= tq:
            pl.when(t >= n_tiles - keep // tq)(lambda: copy(slice(None)))
        else:
            pl.when(t == n_tiles - 1)(lambda: copy(slice(tq - keep, tq)))

    lane = lax.broadcasted_iota(jnp.int32, (group, LANES), 1)

    def lanes_to(x, width):
        return jnp.tile(x, (1, -(-width // LANES)))[:, :width]

    def stat(x):
        return jnp.broadcast_to(x, (x.shape[0], LANES))

    def attend(masked):
        def rows_of(sg):
            return pl.ds(pl.multiple_of(sg * group, group), group)

        def band_of(sg, band):
            return pl.ds(pl.multiple_of(sg * group, group), band)

        def load_q(c, sg):
            return z_ref[c, rows_of(sg), :].astype(BF16)

        def store_gated(c_out, c_gate, sg, o):
            g = z_ref[c_gate, rows_of(sg), :].astype(F32)
            o3_s[c_out, rows_of(sg), :] = (o * (g * jax.nn.sigmoid(g))).astype(BF16)

        def mask_history(s, sg, hist, n_hist):
            if not masked:
                return s
            col = lax.broadcasted_iota(jnp.int32, s.shape, 1)
            return jnp.where(col >= hist - n_hist - (t * tq + sg * group), s, NEG_INF)

        def a_scores(b, slot):
            sg, p = _divmod_pow2(b, N_PAIRS)
            q2 = jnp.concatenate(_split_heads(load_q(QA_C + p, sg)), axis=0)
            k = ka_s[p, band_of(sg, band_a), :]
            s = lax.dot_general(q2, k, _NT, preferred_element_type=F32) + bias_a_ref[p]
            s = mask_history(s, sg, HIST_A, n_hist_a)
            sa_s[slot] = s
            ma_s[slot] = stat(jnp.max(s, axis=-1, keepdims=True))

        def a_softmax(b, slot):
            e = jnp.exp2(sa_s[slot] - lanes_to(ma_s[slot], band_a))
            la_s[slot] = stat(jnp.sum(e, axis=-1, keepdims=True))
            pa_s[slot] = e.astype(BF16)

        def a_values(b, slot):
            sg, p = _divmod_pow2(b, N_PAIRS)
            v = va_s[p, band_of(sg, band_a), :]
            pv = jnp.dot(pa_s[slot], v, preferred_element_type=F32) / la_s[slot]
            store_gated(p, GA_C + p, sg, jnp.where(lane < HEAD_DIM, pv[:group], pv[group:]))

        def sink_rows(g):
            return jnp.concatenate(
                [jnp.full((group, LANES), sinks_ref[GROUP_B * g + i] * LOG2E, F32) for i in range(GROUP_B)],
                axis=0)

        def b_scores(b, slot):
            sg, g = _divmod_pow2(b, N_KV_B)
            q4 = []
            for lb in range(2):
                q4 += _split_heads(load_q(QB_C + 2 * g + lb, sg))
            k = kb_s[g, band_of(sg, band_b), :]
            s = lax.dot_general(jnp.concatenate(q4, axis=0), k, _NT, preferred_element_type=F32) + bias_b_ref[g]
            s = mask_history(s, sg, HIST_B, n_hist_b)
            sb_s[slot] = s
            mb_s[slot] = jnp.maximum(stat(jnp.max(s, axis=-1, keepdims=True)), sink_rows(g))

        def b_softmax(b, slot):
            _, g = _divmod_pow2(b, N_KV_B)
            m = mb_s[slot]
            e = jnp.exp2(sb_s[slot] - lanes_to(m, band_b))
            lb_s[slot] = stat(jnp.sum(e, axis=-1, keepdims=True)) + jnp.exp2(sink_rows(g) - m)
            pb_s[slot] = e.astype(BF16)

        def b_values(b, slot):
            sg, g = _divmod_pow2(b, N_KV_B)
            v = vb_s[g, band_of(sg, band_b), :]
            pv = jnp.dot(pb_s[slot], v, preferred_element_type=F32) / lb_s[slot]
            for lb in range(2):
                top = pv[2 * lb * group:(2 * lb + 1) * group]
                bot = pv[(2 * lb + 1) * group:(2 * lb + 2) * group]
                c = 2 * g + lb
                store_gated(N_PAIRS + c, GB_C + c, sg, jnp.where(lane < HEAD_DIM, top, bot))

        _pipelines((n_groups * N_PAIRS, (a_scores, a_softmax, a_values), 16),
                   (n_groups * N_KV_B, (b_scores, b_softmax, b_values), 8))

    pl.when(t == 0)(init_history)
    for p in range(N_PAIRS):
        ka_s[p, HIST_A:HIST_A + tq, :] = z_ref[KA_C + p].astype(BF16)
        va_s[p, HIST_A:HIST_A + tq, :] = z_ref[VA_C + p].astype(BF16)
    for blk in range(KV_B // LANES):
        for buf, c0 in ((kb_s, KB_C), (vb_s, VB_C)):
            lo, hi = _dup_halves(z_ref[c0 + blk].astype(F32))
            buf[2 * blk, HIST_B:HIST_B + tq, :] = lo
            buf[2 * blk + 1, HIST_B:HIST_B + tq, :] = hi
    write_tail(tak_ref, KA_C, N_PAIRS, keep_a)
    write_tail(tav_ref, VA_C, N_PAIRS, keep_a)
    write_tail(tbk_ref, KB_C, KV_B // LANES, keep_b)
    write_tail(tbv_ref, VB_C, KV_B // LANES, keep_b)

    if n_hist_a < HIST_A or n_hist_b < HIST_B:
        pl.when(t * tq < HIST_A)(lambda: attend(True))
        pl.when(t * tq >= HIST_A)(lambda: attend(False))
    else:
        attend(False)

    for c in range(2 * N_PAIRS):
        o_s[:, c * LANES:(c + 1) * LANES] = o3_s[c]
    d_model = y_s.shape[1]
    tn = 512
    for n0 in range(0, d_model, tn):
        y_s[:, n0:n0 + tn] = x_ref[0, :, n0:n0 + tn] + jnp.dot(
            o_s[...], wout_ref[:, n0:n0 + tn], preferred_element_type=F32)
    y = y_s[...]
    ms = jnp.mean(y * y, axis=-1, keepdims=True)
    y_ref[0] = y * lax.rsqrt(ms + RMS_EPS) * gfin_ref[...]

    if n_tiles > 1:
        for buf, hist in ((ka_s, HIST_A), (va_s, HIST_A), (kb_s, HIST_B), (vb_s, HIST_B)):
            step = min(tq, hist)
            for d0 in range(0, hist, step):
                buf[:, d0:d0 + step, :] = buf[:, d0 + tq:d0 + tq + step, :]


def _mixer(z3, x, caches, bias_a, bias_b, sinks, wout_bf16, gfin, tq, group, keep_a, keep_b):
    nb, s, d = x.shape
    n_tiles = s // tq
    has_cache = caches is not None
    band_a = HIST_A + group
    band_b = HIST_B + group

    def tail_spec(keep, width):
        if keep >= tq:
            nblk = keep // tq
            return pl.BlockSpec((1, tq, width), lambda b, t: (b, jnp.clip(t - (n_tiles - nblk), 0, nblk - 1), 0))
        return pl.BlockSpec((1, keep, width), lambda b, t: (b, 0, 0))

    in_specs = [pl.BlockSpec((N_COLS, tq, LANES), lambda b, t: (0, b * n_tiles + t, 0)),
                pl.BlockSpec((1, tq, d), lambda b, t: (b, t, 0))]
    args = [z3, x]
    if has_cache:
        for c in caches:
            in_specs.append(pl.BlockSpec((1,) + c.shape[1:], lambda b, t: (b, 0, 0)))
            args.append(c)
    in_specs += [
        _resident(bias_a.shape),
        _resident(bias_b.shape),
        pl.BlockSpec(memory_space=pltpu.SMEM),
        _resident(wout_bf16.shape),
        _resident(gfin.shape),
    ]
    args += [bias_a, bias_b, sinks, wout_bf16, gfin]
    kern = functools.partial(
        _mixer_kernel, tq=tq, group=group,
        n_hist_a=HIST_A if has_cache else 0, n_hist_b=HIST_B if has_cache else 0,
        has_cache=has_cache, n_tiles=n_tiles, keep_a=keep_a, keep_b=keep_b)
    return pl.pallas_call(
        kern,
        grid=(nb, n_tiles),
        in_specs=in_specs,
        out_specs=[
            pl.BlockSpec((1, tq, d), lambda b, t: (b, t, 0)),
            tail_spec(keep_a, D_A), tail_spec(keep_a, D_A),
            tail_spec(keep_b, KV_B), tail_spec(keep_b, KV_B),
        ],
        out_shape=[
            jax.ShapeDtypeStruct((nb, s, d), F32),
            jax.ShapeDtypeStruct((nb, keep_a, D_A), F32), jax.ShapeDtypeStruct((nb, keep_a, D_A), F32),
            jax.ShapeDtypeStruct((nb, keep_b, KV_B), F32), jax.ShapeDtypeStruct((nb, keep_b, KV_B), F32),
        ],
        scratch_shapes=[
            pltpu.VMEM((N_PAIRS, HIST_A + tq, LANES), BF16),
            pltpu.VMEM((N_PAIRS, HIST_A + tq, LANES), BF16),
            pltpu.VMEM((N_KV_B, HIST_B + tq, LANES), BF16),
            pltpu.VMEM((N_KV_B, HIST_B + tq, LANES), BF16),
            pltpu.VMEM((2, 2 * group, band_a), F32),
            pltpu.VMEM((2, 2 * group, band_a), BF16),
            pltpu.VMEM((2, 2 * group, LANES), F32),
            pltpu.VMEM((2, 2 * group, LANES), F32),
            pltpu.VMEM((2, GROUP_B * group, band_b), F32),
            pltpu.VMEM((2, GROUP_B * group, band_b), BF16),
            pltpu.VMEM((2, GROUP_B * group, LANES), F32),
            pltpu.VMEM((2, GROUP_B * group, LANES), F32),
            pltpu.VMEM((2 * N_PAIRS, tq, LANES), BF16),
            pltpu.VMEM((tq, D_A + D_B), BF16),
            pltpu.VMEM((tq, d), F32),
        ],
        compiler_params=pltpu.CompilerParams(
            dimension_semantics=("arbitrary", "arbitrary"), vmem_limit_bytes=VMEM_LIMIT_BYTES),
        name=f"mixer_tq{tq}",
    )(*args)


def kernel(x_prompt, x_sample, cache_a_k, cache_a_v, cache_b_k, cache_b_v, norm_in, w_in, rel_bias_a,
           sinks_b, w_out, norm_final):
    b_p, seq, d = x_prompt.shape
    b_s, n_new, _ = x_sample.shape
    assert norm_in.shape[0] == 1, "one layer: the final norm is fused into the mixer call"
    assert n_new == CHUNK and seq % 256 == 0 and seq >= HIST_A
    assert cache_a_k.shape[2] == HIST_A and cache_b_k.shape[2] == HIST_B

    w_in_b = w_in[0].astype(BF16)
    w_out_b = w_out[0].astype(BF16)
    g_in = norm_in[0].reshape(1, d)
    g_fin = norm_final.reshape(1, d)
    sinks = sinks_b[0]

    z_p = _inproj(x_prompt.reshape(b_p * seq, d), g_in, w_in_b, BF16, tm=512)
    y_p, ak_p, av_p, bk_p, bv_p = _mixer(
        z_p, x_prompt, None, _rel_bias(rel_bias_a[0], 128), _alibi_bias(128), sinks, w_out_b, g_fin,
        tq=256, group=128, keep_a=HIST_A, keep_b=HIST_B)

    z_s = _inproj(x_sample.reshape(b_s * n_new, d), g_in, w_in_b, F32, tm=b_s * n_new)
    caches = (cache_a_k[0].reshape(b_s, HIST_A, D_A), cache_a_v[0].reshape(b_s, HIST_A, D_A),
              cache_b_k[0].reshape(b_s, HIST_B, KV_B), cache_b_v[0].reshape(b_s, HIST_B, KV_B))
    y_s, ak_s, av_s, bk_s, bv_s = _mixer(
        z_s, x_sample, caches, _rel_bias(rel_bias_a[0], CHUNK), _alibi_bias(CHUNK), sinks, w_out_b, g_fin,
        tq=CHUNK, group=CHUNK, keep_a=n_new, keep_b=n_new)

    def heads(rows, n_heads):
        return rows.reshape(1, rows.shape[0], rows.shape[1], n_heads, HEAD_DIM)

    def rolled(cache, new, n_heads):
        return jnp.concatenate([cache, heads(new, n_heads)], axis=2)[:, :, n_new:]

    return (y_p, y_s,
            heads(ak_p, N_HEADS), heads(av_p, N_HEADS), heads(bk_p, N_KV_B), heads(bv_p, N_KV_B),
            rolled(cache_a_k, ak_s, N_HEADS), rolled(cache_a_v, av_s, N_HEADS),
            rolled(cache_b_k, bk_s, N_KV_B), rolled(cache_b_v, bv_s, N_KV_B))
```

```python
import functools

import jax
import jax.numpy as jnp
import numpy as np
from jax import lax
from jax.experimental import pallas as pl
from jax.experimental.pallas import tpu as pltpu

CHUNK = 64
HEAD_DIM = 64
N_HEADS = 16
N_KV_B = 4
GROUP_B = N_HEADS // N_KV_B
D_A = N_HEADS * HEAD_DIM
D_B = N_HEADS * HEAD_DIM
KV_B = N_KV_B * HEAD_DIM
HIST_A = 8 * CHUNK
HIST_B = 2 * CHUNK
REL_CLIP = 256
N_REL = REL_CLIP + CHUNK
RMS_EPS = 1e-6
NEG_INF = -1e30
LOG2E = 1.4426950408889634

LANES = 128
N_PAIRS = D_A // LANES
QA_C, KA_C, VA_C, GA_C = 0, N_PAIRS, 2 * N_PAIRS, 3 * N_PAIRS
QB_C = 4 * N_PAIRS
KB_C = QB_C + N_PAIRS
VB_C = KB_C + KV_B // LANES
GB_C = VB_C + KV_B // LANES
N_COLS = GB_C + N_PAIRS
D_IN = N_COLS * LANES

VMEM_LIMIT_BYTES = 56 * 1024 * 1024

BF16 = jnp.bfloat16
F32 = jnp.float32
_NT = (((1,), (1,)), ((), ()))


def _resident(shape):
    return pl.BlockSpec(shape, lambda *_: (0,) * len(shape), pipeline_mode=pl.Buffered(1))


def _rel_bias_kernel(tab_ref, out_ref, *, group, width):
    band = out_ref.shape[2]
    n = lax.broadcasted_iota(jnp.int32, (N_REL, width), 0)
    u = lax.broadcasted_iota(jnp.int32, (N_REL, width), 1)
    idx = jnp.clip(HIST_A + group - 1 - u, -(CHUNK - 1), REL_CLIP) + (CHUNK - 1)
    onehot = (n == idx).astype(F32)
    gv = jnp.dot(tab_ref[...], onehot, precision=lax.Precision.HIGHEST, preferred_element_type=F32) * LOG2E
    r = lax.broadcasted_iota(jnp.int32, (group, band), 0)
    j = lax.broadcasted_iota(jnp.int32, (group, band), 1)
    rel = j - (r // CHUNK) * CHUNK
    valid = (rel >= 0) & (rel < HIST_A + CHUNK)
    for h in range(N_HEADS):
        rows = jnp.broadcast_to(gv[h:h + 1, :], (group, width))
        rows = pltpu.roll(rows, width - group + 1, 1, stride=1, stride_axis=0)
        out_ref[h // 2, (h % 2) * group:(h % 2 + 1) * group, :] = jnp.where(valid, rows[:, :band], NEG_INF)


def _rel_bias(table, group):
    band = HIST_A + group
    width = -(-(HIST_A + 2 * group - 1) // LANES) * LANES
    return pl.pallas_call(
        functools.partial(_rel_bias_kernel, group=group, width=width),
        out_shape=jax.ShapeDtypeStruct((N_PAIRS, 2 * group, band), F32),
        name=f"rel_bias_g{group}",
    )(table)


def _alibi_bias(group):
    r = np.arange(group)[:, None]
    j = np.arange(HIST_B + group)[None, :]
    d = np.abs(HIST_B + r - j).astype(np.float32)
    slopes = (2.0 ** (-8.0 * np.arange(1, N_HEADS + 1, dtype=np.float32) / N_HEADS)).astype(np.float32)
    bias = (-slopes[:, None, None] * d[None]).astype(np.float64) * LOG2E
    rel = j - (r // CHUNK) * CHUNK
    valid = (rel >= 0) & (rel < HIST_B + CHUNK)
    bias = np.where(valid[None], bias, np.float32(NEG_INF)).astype(np.float32)
    return jnp.asarray(bias.reshape(N_KV_B, GROUP_B * group, HIST_B + group))


def _inproj_kernel(x_ref, g_ref, w_ref, z_ref, h_ref, *, tn):
    x = x_ref[...]
    ms = jnp.mean(x * x, axis=-1, keepdims=True)
    h_ref[...] = (x * lax.rsqrt(ms + RMS_EPS) * g_ref[...]).astype(h_ref.dtype)
    for n0 in range(0, w_ref.shape[1], tn):
        res = jnp.dot(h_ref[...], w_ref[:, n0:n0 + tn], preferred_element_type=F32)
        c0 = n0 // LANES
        if c0 in Q_CHUNKS:
            res = res * (HEAD_DIM ** -0.5 * LOG2E)
        res = res.astype(z_ref.dtype)
        for j in range(tn // LANES):
            z_ref[c0 + j] = res[:, j * LANES:(j + 1) * LANES]


INPROJ_TN = 4 * LANES
Q_CHUNKS = tuple(c for c0 in (QA_C, QB_C) for c in range(c0, c0 + N_PAIRS, INPROJ_TN // LANES))


def _inproj(x2d, gain, w_bf16, out_dtype, tm):
    m, d = x2d.shape
    return pl.pallas_call(
        functools.partial(_inproj_kernel, tn=INPROJ_TN),
        grid=(m // tm,),
        in_specs=[
            pl.BlockSpec((tm, d), lambda i: (i, 0)),
            _resident((1, d)),
            _resident((d, D_IN)),
        ],
        out_specs=pl.BlockSpec((N_COLS, tm, LANES), lambda i: (0, i, 0)),
        out_shape=jax.ShapeDtypeStruct((N_COLS, m, LANES), out_dtype),
        scratch_shapes=[pltpu.VMEM((tm, d), BF16)],
        compiler_params=pltpu.CompilerParams(
            dimension_semantics=("arbitrary",), vmem_limit_bytes=VMEM_LIMIT_BYTES),
        name=f"inproj_m{m}",
    )(x2d, gain, w_bf16)


def _dup_halves(x):
    lane = lax.broadcasted_iota(jnp.int32, x.shape, 1)
    rolled = pltpu.roll(x, HEAD_DIM, 1)
    return (jnp.where(lane < HEAD_DIM, x, rolled).astype(BF16),
            jnp.where(lane < HEAD_DIM, rolled, x).astype(BF16))


def _split_heads(q):
    lane = lax.broadcasted_iota(jnp.int32, q.shape, 1)
    zero = jnp.zeros_like(q)
    return [jnp.where(lane < HEAD_DIM, q, zero), jnp.where(lane >= HEAD_DIM, q, zero)]


N_SLOTS = 2


def _pipeline(n_blocks, stage1, stage2, stage3):
    for b in range(n_blocks + 2):
        if b < n_blocks:
            stage1(b, b % N_SLOTS)
        if 0 <= b - 1 < n_blocks:
            stage2(b - 1, (b - 1) % N_SLOTS)
        if 0 <= b - 2 < n_blocks:
            stage3(b - 2, (b - 2) % N_SLOTS)


def _mixer_kernel(*refs, tq, group, n_hist_a, n_hist_b, has_cache, n_tiles, keep_a, keep_b):
    if has_cache:
        z_ref, x_ref, cak_ref, cav_ref, cbk_ref, cbv_ref = refs[:6]
        refs = refs[6:]
    else:
        z_ref, x_ref = refs[:2]
        refs = refs[2:]
    (bias_a_ref, bias_b_ref, sinks_ref, wout_ref, gfin_ref,
     y_ref, tak_ref, tav_ref, tbk_ref, tbv_ref,
     ka_s, va_s, kb_s, vb_s, sa_s, pa_s, ma_s, la_s, sb_s, pb_s, mb_s, lb_s, o_s, y_s) = refs
    t = pl.program_id(1)
    band_a = HIST_A + group
    band_b = HIST_B + group
    n_groups = tq // group

    def init_history():
        for p in range(N_PAIRS):
            if has_cache:
                ka_s[p, 0:HIST_A, :] = cak_ref[0, :, p * LANES:(p + 1) * LANES].astype(BF16)
                va_s[p, 0:HIST_A, :] = cav_ref[0, :, p * LANES:(p + 1) * LANES].astype(BF16)
            else:
                ka_s[p, 0:HIST_A, :] = jnp.zeros((HIST_A, LANES), BF16)
                va_s[p, 0:HIST_A, :] = jnp.zeros((HIST_A, LANES), BF16)
        for blk in range(KV_B // LANES):
            for buf, cache in ((kb_s, cbk_ref if has_cache else None), (vb_s, cbv_ref if has_cache else None)):
                if has_cache:
                    lo, hi = _dup_halves(cache[0, :, blk * LANES:(blk + 1) * LANES])
                else:
                    lo = hi = jnp.zeros((HIST_B, LANES), BF16)
                buf[2 * blk, 0:HIST_B, :] = lo
                buf[2 * blk + 1, 0:HIST_B, :] = hi

    def write_tail(ref, c0, n_cblk, keep):
        def copy(rows):
            for p in range(n_cblk):
                ref[0, :, p * LANES:(p + 1) * LANES] = z_ref[c0 + p, rows, :].astype(F32)
        if keep >= tq:
            pl.when(t >= n_tiles - keep // tq)(lambda: copy(slice(None)))
        else:
            pl.when(t == n_tiles - 1)(lambda: copy(slice(tq - keep, tq)))

    lane = lax.broadcasted_iota(jnp.int32, (group, LANES), 1)

    def lanes_to(x, width):
        return jnp.tile(x, (1, -(-width // LANES)))[:, :width]

    def stat(x):
        return jnp.broadcast_to(x, (x.shape[0], LANES))

    def attend(masked):
        for sg in range(n_groups):
            attend_group(sg, masked)

    def attend_group(sg, masked):
        rows = slice(sg * group, (sg + 1) * group)
        rows_a = slice(sg * group, sg * group + band_a)
        rows_b = slice(sg * group, sg * group + band_b)

        def load_q(c):
            return z_ref[c, rows, :].astype(BF16)

        def store_gated(c_out, c_gate, o):
            g = z_ref[c_gate, rows, :].astype(F32)
            o_s[rows, c_out * LANES:(c_out + 1) * LANES] = (o * (g * jax.nn.sigmoid(g))).astype(BF16)

        def mask_history(s, hist, n_hist):
            if not masked:
                return s
            col = lax.broadcasted_iota(jnp.int32, s.shape, 1)
            return jnp.where(col >= hist - n_hist - (t * tq + sg * group), s, NEG_INF)

        def a_scores(p, slot):
            q2 = jnp.concatenate(_split_heads(load_q(QA_C + p)), axis=0)
            s = lax.dot_general(q2, ka_s[p, rows_a, :], _NT, preferred_element_type=F32) + bias_a_ref[p]
            s = mask_history(s, HIST_A, n_hist_a)
            sa_s[slot] = s
            ma_s[slot] = stat(jnp.max(s, axis=-1, keepdims=True))

        def a_softmax(p, slot):
            e = jnp.exp2(sa_s[slot] - lanes_to(ma_s[slot], band_a))
            la_s[slot] = stat(jnp.sum(e, axis=-1, keepdims=True))
            pa_s[slot] = e.astype(BF16)

        def a_values(p, slot):
            pv = jnp.dot(pa_s[slot], va_s[p, rows_a, :], preferred_element_type=F32) / la_s[slot]
            store_gated(p, GA_C + p, jnp.where(lane < HEAD_DIM, pv[:group], pv[group:]))

        def sink_rows(g):
            return jnp.concatenate(
                [jnp.full((group, LANES), sinks_ref[GROUP_B * g + i] * LOG2E, F32) for i in range(GROUP_B)],
                axis=0)

        def b_scores(g, slot):
            q4 = []
            for lb in range(2):
                q4 += _split_heads(load_q(QB_C + 2 * g + lb))
            s = lax.dot_general(jnp.concatenate(q4, axis=0), kb_s[g, rows_b, :], _NT,
                                preferred_element_type=F32) + bias_b_ref[g]
            s = mask_history(s, HIST_B, n_hist_b)
            sb_s[slot] = s
            mb_s[slot] = jnp.maximum(stat(jnp.max(s, axis=-1, keepdims=True)), sink_rows(g))

        def b_softmax(g, slot):
            m = mb_s[slot]
            e = jnp.exp2(sb_s[slot] - lanes_to(m, band_b))
            lb_s[slot] = stat(jnp.sum(e, axis=-1, keepdims=True)) + jnp.exp2(sink_rows(g) - m)
            pb_s[slot] = e.astype(BF16)

        def b_values(g, slot):
            pv = jnp.dot(pb_s[slot], vb_s[g, rows_b, :], preferred_element_type=F32) / lb_s[slot]
            for lb in range(2):
                top = pv[2 * lb * group:(2 * lb + 1) * group]
                bot = pv[(2 * lb + 1) * group:(2 * lb + 2) * group]
                c = 2 * g + lb
                store_gated(N_PAIRS + c, GB_C + c, jnp.where(lane < HEAD_DIM, top, bot))

        _pipeline(N_PAIRS, a_scores, a_softmax, a_values)
        _pipeline(N_KV_B, b_scores, b_softmax, b_values)

        d_model = y_s.shape[1]
        tn = 512
        for n0 in range(0, d_model, tn):
            y_s[rows, n0:n0 + tn] = x_ref[0, rows, n0:n0 + tn] + jnp.dot(
                o_s[rows, :], wout_ref[:, n0:n0 + tn], preferred_element_type=F32)
        y = y_s[rows, :]
        ms = jnp.mean(y * y, axis=-1, keepdims=True)
        y_ref[0, rows, :] = y * lax.rsqrt(ms + RMS_EPS) * gfin_ref[...]

    pl.when(t == 0)(init_history)
    for p in range(N_PAIRS):
        ka_s[p, HIST_A:HIST_A + tq, :] = z_ref[KA_C + p].astype(BF16)
        va_s[p, HIST_A:HIST_A + tq, :] = z_ref[VA_C + p].astype(BF16)
    for blk in range(KV_B // LANES):
        for buf, c0 in ((kb_s, KB_C), (vb_s, VB_C)):
            lo, hi = _dup_halves(z_ref[c0 + blk].astype(F32))
            buf[2 * blk, HIST_B:HIST_B + tq, :] = lo
            buf[2 * blk + 1, HIST_B:HIST_B + tq, :] = hi
    write_tail(tak_ref, KA_C, N_PAIRS, keep_a)
    write_tail(tav_ref, VA_C, N_PAIRS, keep_a)
    write_tail(tbk_ref, KB_C, KV_B // LANES, keep_b)
    write_tail(tbv_ref, VB_C, KV_B // LANES, keep_b)

    if n_hist_a < HIST_A or n_hist_b < HIST_B:
        pl.when(t * tq < HIST_A)(lambda: attend(True))
        pl.when(t * tq >= HIST_A)(lambda: attend(False))
    else:
        attend(False)

    if n_tiles > 1:
        for buf, hist in ((ka_s, HIST_A), (va_s, HIST_A), (kb_s, HIST_B), (vb_s, HIST_B)):
            step = min(tq, hist)
            for d0 in range(0, hist, step):
                buf[:, d0:d0 + step, :] = buf[:, d0 + tq:d0 + tq + step, :]


def _mixer(z3, x, caches, bias_a, bias_b, sinks, wout_bf16, gfin, tq, group, keep_a, keep_b):
    nb, s, d = x.shape
    n_tiles = s // tq
    has_cache = caches is not None
    band_a = HIST_A + group
    band_b = HIST_B + group

    def tail_spec(keep, width):
        if keep >= tq:
            nblk = keep // tq
            return pl.BlockSpec((1, tq, width), lambda b, t: (b, jnp.clip(t - (n_tiles - nblk), 0, nblk - 1), 0))
        return pl.BlockSpec((1, keep, width), lambda b, t: (b, 0, 0))

    in_specs = [pl.BlockSpec((N_COLS, tq, LANES), lambda b, t: (0, b * n_tiles + t, 0)),
                pl.BlockSpec((1, tq, d), lambda b, t: (b, t, 0))]
    args = [z3, x]
    if has_cache:
        for c in caches:
            in_specs.append(pl.BlockSpec((1,) + c.shape[1:], lambda b, t: (b, 0, 0)))
            args.append(c)
    in_specs += [
        _resident(bias_a.shape),
        _resident(bias_b.shape),
        pl.BlockSpec(memory_space=pltpu.SMEM),
        _resident(wout_bf16.shape),
        _resident(gfin.shape),
    ]
    args += [bias_a, bias_b, sinks, wout_bf16, gfin]
    kern = functools.partial(
        _mixer_kernel, tq=tq, group=group,
        n_hist_a=HIST_A if has_cache else 0, n_hist_b=HIST_B if has_cache else 0,
        has_cache=has_cache, n_tiles=n_tiles, keep_a=keep_a, keep_b=keep_b)
    return pl.pallas_call(
        kern,
        grid=(nb, n_tiles),
        in_specs=in_specs,
        out_specs=[
            pl.BlockSpec((1, tq, d), lambda b, t: (b, t, 0)),
            tail_spec(keep_a, D_A), tail_spec(keep_a, D_A),
            tail_spec(keep_b, KV_B), tail_spec(keep_b, KV_B),
        ],
        out_shape=[
            jax.ShapeDtypeStruct((nb, s, d), F32),
            jax.ShapeDtypeStruct((nb, keep_a, D_A), F32), jax.ShapeDtypeStruct((nb, keep_a, D_A), F32),
            jax.ShapeDtypeStruct((nb, keep_b, KV_B), F32), jax.ShapeDtypeStruct((nb, keep_b, KV_B), F32),
        ],
        scratch_shapes=[
            pltpu.VMEM((N_PAIRS, HIST_A + tq, LANES), BF16),
            pltpu.VMEM((N_PAIRS, HIST_A + tq, LANES), BF16),
            pltpu.VMEM((N_KV_B, HIST_B + tq, LANES), BF16),
            pltpu.VMEM((N_KV_B, HIST_B + tq, LANES), BF16),
            pltpu.VMEM((N_SLOTS, 2 * group, band_a), F32),
            pltpu.VMEM((N_SLOTS, 2 * group, band_a), BF16),
            pltpu.VMEM((N_SLOTS, 2 * group, LANES), F32),
            pltpu.VMEM((N_SLOTS, 2 * group, LANES), F32),
            pltpu.VMEM((N_SLOTS, GROUP_B * group, band_b), F32),
            pltpu.VMEM((N_SLOTS, GROUP_B * group, band_b), BF16),
            pltpu.VMEM((N_SLOTS, GROUP_B * group, LANES), F32),
            pltpu.VMEM((N_SLOTS, GROUP_B * group, LANES), F32),
            pltpu.VMEM((tq, D_A + D_B), BF16),
            pltpu.VMEM((tq, d), F32),
        ],
        compiler_params=pltpu.CompilerParams(
            dimension_semantics=("arbitrary", "arbitrary"), vmem_limit_bytes=VMEM_LIMIT_BYTES),
        name=f"mixer_tq{tq}",
    )(*args)


def kernel(x_prompt, x_sample, cache_a_k, cache_a_v, cache_b_k, cache_b_v, norm_in, w_in, rel_bias_a,
           sinks_b, w_out, norm_final):
    b_p, seq, d = x_prompt.shape
    b_s, n_new, _ = x_sample.shape
    assert norm_in.shape[0] == 1, "one layer: the final norm is fused into the mixer call"
    assert n_new == CHUNK and seq % 256 == 0 and seq >= HIST_A
    assert cache_a_k.shape[2] == HIST_A and cache_b_k.shape[2] == HIST_B

    w_in_b = w_in[0].astype(BF16)
    w_out_b = w_out[0].astype(BF16)
    g_in = norm_in[0].reshape(1, d)
    g_fin = norm_final.reshape(1, d)
    sinks = sinks_b[0]

    z_p = _inproj(x_prompt.reshape(b_p * seq, d), g_in, w_in_b, BF16, tm=512)
    y_p, ak_p, av_p, bk_p, bv_p = _mixer(
        z_p, x_prompt, None, _rel_bias(rel_bias_a[0], 128), _alibi_bias(128), sinks, w_out_b, g_fin,
        tq=256, group=128, keep_a=HIST_A, keep_b=HIST_B)

    z_s = _inproj(x_sample.reshape(b_s * n_new, d), g_in, w_in_b, F32, tm=b_s * n_new)
    caches = (cache_a_k[0].reshape(b_s, HIST_A, D_A), cache_a_v[0].reshape(b_s, HIST_A, D_A),
              cache_b_k[0].reshape(b_s, HIST_B, KV_B), cache_b_v[0].reshape(b_s, HIST_B, KV_B))
    y_s, ak_s, av_s, bk_s, bv_s = _mixer(
        z_s, x_sample, caches, _rel_bias(rel_bias_a[0], CHUNK), _alibi_bias(CHUNK), sinks, w_out_b, g_fin,
        tq=CHUNK, group=CHUNK, keep_a=n_new, keep_b=n_new)

    def heads(rows, n_heads):
        return rows.reshape(1, rows.shape[0], rows.shape[1], n_heads, HEAD_DIM)

    def rolled(cache, new, n_heads):
        return jnp.concatenate([cache, heads(new, n_heads)], axis=2)[:, :, n_new:]

    return (y_p, y_s,
            heads(ak_p, N_HEADS), heads(av_p, N_HEADS), heads(bk_p, N_KV_B), heads(bv_p, N_KV_B),
            rolled(cache_a_k, ak_s, N_HEADS), rolled(cache_a_v, av_s, N_HEADS),
            rolled(cache_b_k, bk_s, N_KV_B), rolled(cache_b_v, bv_s, N_KV_B))
```

```python
import functools

import jax
import jax.numpy as jnp
import numpy as np
from jax import lax
from jax.experimental import pallas as pl
from jax.experimental.pallas import tpu as pltpu

CHUNK = 64
HEAD_DIM = 64
N_HEADS = 16
N_KV_B = 4
GROUP_B = N_HEADS // N_KV_B
D_A = N_HEADS * HEAD_DIM
D_B = N_HEADS * HEAD_DIM
KV_B = N_KV_B * HEAD_DIM
HIST_A = 8 * CHUNK
HIST_B = 2 * CHUNK
REL_CLIP = 256
N_REL = REL_CLIP + CHUNK
RMS_EPS = 1e-6
NEG_INF = -1e30
LOG2E = 1.4426950408889634

LANES = 128
N_PAIRS = D_A // LANES
QA_C, KA_C, VA_C, GA_C = 0, N_PAIRS, 2 * N_PAIRS, 3 * N_PAIRS
QB_C = 4 * N_PAIRS
KB_C = QB_C + N_PAIRS
VB_C = KB_C + KV_B // LANES
GB_C = VB_C + KV_B // LANES
N_COLS = GB_C + N_PAIRS
D_IN = N_COLS * LANES

VMEM_LIMIT_BYTES = 56 * 1024 * 1024

BF16 = jnp.bfloat16
F32 = jnp.float32
_NT = (((1,), (1,)), ((), ()))


def _resident(shape):
    return pl.BlockSpec(shape, lambda *_: (0,) * len(shape), pipeline_mode=pl.Buffered(1))


def _rel_bias_kernel(tab_ref, out_ref, *, group, width):
    band = out_ref.shape[2]
    n = lax.broadcasted_iota(jnp.int32, (N_REL, width), 0)
    u = lax.broadcasted_iota(jnp.int32, (N_REL, width), 1)
    idx = jnp.clip(HIST_A + group - 1 - u, -(CHUNK - 1), REL_CLIP) + (CHUNK - 1)
    onehot = (n == idx).astype(F32)
    gv = jnp.dot(tab_ref[...], onehot, precision=lax.Precision.HIGHEST, preferred_element_type=F32) * LOG2E
    r = lax.broadcasted_iota(jnp.int32, (group, band), 0)
    j = lax.broadcasted_iota(jnp.int32, (group, band), 1)
    rel = j - (r // CHUNK) * CHUNK
    valid = (rel >= 0) & (rel < HIST_A + CHUNK)
    for h in range(N_HEADS):
        rows = jnp.broadcast_to(gv[h:h + 1, :], (group, width))
        rows = pltpu.roll(rows, width - group + 1, 1, stride=1, stride_axis=0)
        out_ref[h // 2, (h % 2) * group:(h % 2 + 1) * group, :] = jnp.where(valid, rows[:, :band], NEG_INF)


def _rel_bias(table, group):
    band = HIST_A + group
    width = -(-(HIST_A + 2 * group - 1) // LANES) * LANES
    return pl.pallas_call(
        functools.partial(_rel_bias_kernel, group=group, width=width),
        out_shape=jax.ShapeDtypeStruct((N_PAIRS, 2 * group, band), F32),
        name=f"rel_bias_g{group}",
    )(table)


def _alibi_bias(group):
    r = np.arange(group)[:, None]
    j = np.arange(HIST_B + group)[None, :]
    d = np.abs(HIST_B + r - j).astype(np.float32)
    slopes = (2.0 ** (-8.0 * np.arange(1, N_HEADS + 1, dtype=np.float32) / N_HEADS)).astype(np.float32)
    bias = (-slopes[:, None, None] * d[None]).astype(np.float64) * LOG2E
    rel = j - (r // CHUNK) * CHUNK
    valid = (rel >= 0) & (rel < HIST_B + CHUNK)
    bias = np.where(valid[None], bias, np.float32(NEG_INF)).astype(np.float32)
    return jnp.asarray(np.swapaxes(bias.reshape(N_KV_B, GROUP_B * group, HIST_B + group), 1, 2))


def _inproj_kernel(x_ref, g_ref, w_ref, z_ref, h_ref, *, tn):
    x = x_ref[...]
    ms = jnp.mean(x * x, axis=-1, keepdims=True)
    h_ref[...] = (x * lax.rsqrt(ms + RMS_EPS) * g_ref[...]).astype(h_ref.dtype)
    for n0 in range(0, w_ref.shape[1], tn):
        res = jnp.dot(h_ref[...], w_ref[:, n0:n0 + tn], preferred_element_type=F32)
        c0 = n0 // LANES
        if c0 in Q_CHUNKS:
            res = res * (HEAD_DIM ** -0.5 * LOG2E)
        res = res.astype(z_ref.dtype)
        for j in range(tn // LANES):
            z_ref[c0 + j] = res[:, j * LANES:(j + 1) * LANES]


INPROJ_TN = 4 * LANES
Q_CHUNKS = tuple(c for c0 in (QA_C, QB_C) for c in range(c0, c0 + N_PAIRS, INPROJ_TN // LANES))


def _inproj(x2d, gain, w_bf16, out_dtype, tm):
    m, d = x2d.shape
    return pl.pallas_call(
        functools.partial(_inproj_kernel, tn=INPROJ_TN),
        grid=(m // tm,),
        in_specs=[
            pl.BlockSpec((tm, d), lambda i: (i, 0)),
            _resident((1, d)),
            _resident((d, D_IN)),
        ],
        out_specs=pl.BlockSpec((N_COLS, tm, LANES), lambda i: (0, i, 0)),
        out_shape=jax.ShapeDtypeStruct((N_COLS, m, LANES), out_dtype),
        scratch_shapes=[pltpu.VMEM((tm, d), BF16)],
        compiler_params=pltpu.CompilerParams(
            dimension_semantics=("arbitrary",), vmem_limit_bytes=VMEM_LIMIT_BYTES),
        name=f"inproj_m{m}",
    )(x2d, gain, w_bf16)


def _dup_halves(x):
    lane = lax.broadcasted_iota(jnp.int32, x.shape, 1)
    rolled = pltpu.roll(x, HEAD_DIM, 1)
    return (jnp.where(lane < HEAD_DIM, x, rolled).astype(BF16),
            jnp.where(lane < HEAD_DIM, rolled, x).astype(BF16))


def _split_heads(q):
    lane = lax.broadcasted_iota(jnp.int32, q.shape, 1)
    zero = jnp.zeros_like(q)
    return [jnp.where(lane < HEAD_DIM, q, zero), jnp.where(lane >= HEAD_DIM, q, zero)]


N_SLOTS = 2


def _pipeline(n_blocks, stage1, stage2, stage3):
    for b in range(n_blocks + 2):
        if b < n_blocks:
            stage1(b, b % N_SLOTS)
        if 0 <= b - 1 < n_blocks:
            stage2(b - 1, (b - 1) % N_SLOTS)
        if 0 <= b - 2 < n_blocks:
            stage3(b - 2, (b - 2) % N_SLOTS)


def _mixer_kernel(*refs, tq, group, n_hist_a, n_hist_b, has_cache, n_tiles, keep_a, keep_b):
    if has_cache:
        z_ref, x_ref, cak_ref, cav_ref, cbk_ref, cbv_ref = refs[:6]
        refs = refs[6:]
    else:
        z_ref, x_ref = refs[:2]
        refs = refs[2:]
    (bias_a_ref, bias_b_ref, sinks_ref, wout_ref, gfin_ref,
     y_ref, tak_ref, tav_ref, tbk_ref, tbv_ref,
     ka_s, vat_s, kb_s, vbt_s, sa_s, pa_s, ma_s, la_s, sb_s, pb_s, mb_s, lb_s, o_s, y_s) = refs
    t = pl.program_id(1)
    band_a = HIST_A + group
    band_b = HIST_B + group
    n_groups = tq // group

    def transposed(v):
        return v.astype(F32).T.astype(BF16)

    def init_history():
        for p in range(N_PAIRS):
            if has_cache:
                ka_s[p, 0:HIST_A, :] = cak_ref[0, :, p * LANES:(p + 1) * LANES].astype(BF16)
                vat_s[p, :, 0:HIST_A] = transposed(cav_ref[0, :, p * LANES:(p + 1) * LANES])
            else:
                ka_s[p, 0:HIST_A, :] = jnp.zeros((HIST_A, LANES), BF16)
                vat_s[p, :, 0:HIST_A] = jnp.zeros((LANES, HIST_A), BF16)
        for blk in range(KV_B // LANES):
            if has_cache:
                k_lo, k_hi = _dup_halves(cbk_ref[0, :, blk * LANES:(blk + 1) * LANES])
                v_lo, v_hi = _dup_halves(cbv_ref[0, :, blk * LANES:(blk + 1) * LANES])
            else:
                k_lo = k_hi = v_lo = v_hi = jnp.zeros((HIST_B, LANES), BF16)
            kb_s[2 * blk, 0:HIST_B, :] = k_lo
            kb_s[2 * blk + 1, 0:HIST_B, :] = k_hi
            vbt_s[2 * blk, :, 0:HIST_B] = transposed(v_lo)
            vbt_s[2 * blk + 1, :, 0:HIST_B] = transposed(v_hi)

    def write_tail(ref, c0, n_cblk, keep):
        def copy(rows):
            for p in range(n_cblk):
                ref[0, :, p * LANES:(p + 1) * LANES] = z_ref[c0 + p, rows, :].astype(F32)
        if keep >= tq:
            pl.when(t >= n_tiles - keep // tq)(lambda: copy(slice(None)))
        else:
            pl.when(t == n_tiles - 1)(lambda: copy(slice(tq - keep, tq)))

    def attend(masked):
        for sg in range(n_groups):
            attend_group(sg, masked)

    def attend_group(sg, masked):
        rows = slice(sg * group, (sg + 1) * group)
        rows_a = slice(sg * group, sg * group + band_a)
        rows_b = slice(sg * group, sg * group + band_b)

        def load_q(c):
            return z_ref[c, rows, :].astype(BF16)

        def store_gated(c_out, c_gate, o):
            g = z_ref[c_gate, rows, :].astype(F32)
            o_s[rows, c_out * LANES:(c_out + 1) * LANES] = (o * (g * jax.nn.sigmoid(g))).astype(BF16)

        def mask_rows(st, hist, n_hist):
            if not masked:
                return st
            row = lax.broadcasted_iota(jnp.int32, st.shape, 0)
            return jnp.where(row >= hist - n_hist - (t * tq + sg * group), st, NEG_INF)

        def head_rows(ot):
            row = lax.broadcasted_iota(jnp.int32, (LANES, group), 0)
            return jnp.where(row < HEAD_DIM, ot[:, :group], ot[:, group:]).T

        def a_scores(p, slot):
            q2 = jnp.concatenate(_split_heads(load_q(QA_C + p)), axis=0)
            st = lax.dot_general(ka_s[p, rows_a, :], q2, _NT, preferred_element_type=F32) + bias_a_ref[p]
            st = mask_rows(st, HIST_A, n_hist_a)
            sa_s[slot] = st
            ma_s[slot] = jnp.max(st, axis=0, keepdims=True)

        def a_softmax(p, slot):
            e = jnp.exp2(sa_s[slot] - ma_s[slot])
            la_s[slot] = jnp.sum(e, axis=0, keepdims=True)
            pa_s[slot] = e.astype(BF16)

        def a_values(p, slot):
            ot = jnp.dot(vat_s[p, :, rows_a], pa_s[slot], preferred_element_type=F32) / la_s[slot]
            store_gated(p, GA_C + p, head_rows(ot))

        def sink_row(g):
            return jnp.concatenate(
                [jnp.full((1, group), sinks_ref[GROUP_B * g + i] * LOG2E, F32) for i in range(GROUP_B)], axis=1)

        def b_scores(g, slot):
            q4 = []
            for lb in range(2):
                q4 += _split_heads(load_q(QB_C + 2 * g + lb))
            st = lax.dot_general(kb_s[g, rows_b, :], jnp.concatenate(q4, axis=0), _NT,
                                 preferred_element_type=F32) + bias_b_ref[g]
            st = mask_rows(st, HIST_B, n_hist_b)
            sb_s[slot] = st
            mb_s[slot] = jnp.maximum(jnp.max(st, axis=0, keepdims=True), sink_row(g))

        def b_softmax(g, slot):
            m = mb_s[slot]
            e = jnp.exp2(sb_s[slot] - m)
            lb_s[slot] = jnp.sum(e, axis=0, keepdims=True) + jnp.exp2(sink_row(g) - m)
            pb_s[slot] = e.astype(BF16)

        def b_values(g, slot):
            ot = jnp.dot(vbt_s[g, :, rows_b], pb_s[slot], preferred_element_type=F32) / lb_s[slot]
            for lb in range(2):
                c = 2 * g + lb
                store_gated(N_PAIRS + c, GB_C + c, head_rows(ot[:, 2 * lb * group:(2 * lb + 2) * group]))

        _pipeline(N_PAIRS, a_scores, a_softmax, a_values)
        _pipeline(N_KV_B, b_scores, b_softmax, b_values)

        d_model = y_s.shape[1]
        tn = 512
        for n0 in range(0, d_model, tn):
            y_s[rows, n0:n0 + tn] = x_ref[0, rows, n0:n0 + tn] + jnp.dot(
                o_s[rows, :], wout_ref[:, n0:n0 + tn], preferred_element_type=F32)
        y = y_s[rows, :]
        ms = jnp.mean(y * y, axis=-1, keepdims=True)
        y_ref[0, rows, :] = y * lax.rsqrt(ms + RMS_EPS) * gfin_ref[...]

    pl.when(t == 0)(init_history)
    for p in range(N_PAIRS):
        ka_s[p, HIST_A:HIST_A + tq, :] = z_ref[KA_C + p].astype(BF16)
        vat_s[p, :, HIST_A:HIST_A + tq] = transposed(z_ref[VA_C + p])
    for blk in range(KV_B // LANES):
        k_lo, k_hi = _dup_halves(z_ref[KB_C + blk].astype(F32))
        v_lo, v_hi = _dup_halves(z_ref[VB_C + blk].astype(F32))
        kb_s[2 * blk, HIST_B:HIST_B + tq, :] = k_lo
        kb_s[2 * blk + 1, HIST_B:HIST_B + tq, :] = k_hi
        vbt_s[2 * blk, :, HIST_B:HIST_B + tq] = transposed(v_lo)
        vbt_s[2 * blk + 1, :, HIST_B:HIST_B + tq] = transposed(v_hi)
    write_tail(tak_ref, KA_C, N_PAIRS, keep_a)
    write_tail(tav_ref, VA_C, N_PAIRS, keep_a)
    write_tail(tbk_ref, KB_C, KV_B // LANES, keep_b)
    write_tail(tbv_ref, VB_C, KV_B // LANES, keep_b)

    if n_hist_a < HIST_A or n_hist_b < HIST_B:
        pl.when(t * tq < HIST_A)(lambda: attend(True))
        pl.when(t * tq >= HIST_A)(lambda: attend(False))
    else:
        attend(False)

    if n_tiles > 1:
        for kbuf, vtbuf, hist in ((ka_s, vat_s, HIST_A), (kb_s, vbt_s, HIST_B)):
            step = min(tq, hist)
            for d0 in range(0, hist, step):
                kbuf[:, d0:d0 + step, :] = kbuf[:, d0 + tq:d0 + tq + step, :]
                vtbuf[:, :, d0:d0 + step] = vtbuf[:, :, d0 + tq:d0 + tq + step]


def _mixer(z3, x, caches, bias_a, bias_b, sinks, wout_bf16, gfin, tq, group, keep_a, keep_b):
    nb, s, d = x.shape
    n_tiles = s // tq
    has_cache = caches is not None
    band_a = HIST_A + group
    band_b = HIST_B + group

    def tail_spec(keep, width):
        if keep >= tq:
            nblk = keep // tq
            return pl.BlockSpec((1, tq, width), lambda b, t: (b, jnp.clip(t - (n_tiles - nblk), 0, nblk - 1), 0))
        return pl.BlockSpec((1, keep, width), lambda b, t: (b, 0, 0))

    in_specs = [pl.BlockSpec((N_COLS, tq, LANES), lambda b, t: (0, b * n_tiles + t, 0)),
                pl.BlockSpec((1, tq, d), lambda b, t: (b, t, 0))]
    args = [z3, x]
    if has_cache:
        for c in caches:
            in_specs.append(pl.BlockSpec((1,) + c.shape[1:], lambda b, t: (b, 0, 0)))
            args.append(c)
    in_specs += [
        _resident(bias_a.shape),
        _resident(bias_b.shape),
        pl.BlockSpec(memory_space=pltpu.SMEM),
        _resident(wout_bf16.shape),
        _resident(gfin.shape),
    ]
    args += [bias_a, bias_b, sinks, wout_bf16, gfin]
    kern = functools.partial(
        _mixer_kernel, tq=tq, group=group,
        n_hist_a=HIST_A if has_cache else 0, n_hist_b=HIST_B if has_cache else 0,
        has_cache=has_cache, n_tiles=n_tiles, keep_a=keep_a, keep_b=keep_b)
    return pl.pallas_call(
        kern,
        grid=(nb, n_tiles),
        in_specs=in_specs,
        out_specs=[
            pl.BlockSpec((1, tq, d), lambda b, t: (b, t, 0)),
            tail_spec(keep_a, D_A), tail_spec(keep_a, D_A),
            tail_spec(keep_b, KV_B), tail_spec(keep_b, KV_B),
        ],
        out_shape=[
            jax.ShapeDtypeStruct((nb, s, d), F32),
            jax.ShapeDtypeStruct((nb, keep_a, D_A), F32), jax.ShapeDtypeStruct((nb, keep_a, D_A), F32),
            jax.ShapeDtypeStruct((nb, keep_b, KV_B), F32), jax.ShapeDtypeStruct((nb, keep_b, KV_B), F32),
        ],
        scratch_shapes=[
            pltpu.VMEM((N_PAIRS, HIST_A + tq, LANES), BF16),
            pltpu.VMEM((N_PAIRS, LANES, HIST_A + tq), BF16),
            pltpu.VMEM((N_KV_B, HIST_B + tq, LANES), BF16),
            pltpu.VMEM((N_KV_B, LANES, HIST_B + tq), BF16),
            pltpu.VMEM((N_SLOTS, band_a, 2 * group), F32),
            pltpu.VMEM((N_SLOTS, band_a, 2 * group), BF16),
            pltpu.VMEM((N_SLOTS, 1, 2 * group), F32),
            pltpu.VMEM((N_SLOTS, 1, 2 * group), F32),
            pltpu.VMEM((N_SLOTS, band_b, GROUP_B * group), F32),
            pltpu.VMEM((N_SLOTS, band_b, GROUP_B * group), BF16),
            pltpu.VMEM((N_SLOTS, 1, GROUP_B * group), F32),
            pltpu.VMEM((N_SLOTS, 1, GROUP_B * group), F32),
            pltpu.VMEM((tq, D_A + D_B), BF16),
            pltpu.VMEM((tq, d), F32),
        ],
        compiler_params=pltpu.CompilerParams(
            dimension_semantics=("arbitrary", "arbitrary"), vmem_limit_bytes=VMEM_LIMIT_BYTES),
        name=f"mixer_tq{tq}",
    )(*args)


def kernel(x_prompt, x_sample, cache_a_k, cache_a_v, cache_b_k, cache_b_v, norm_in, w_in, rel_bias_a,
           sinks_b, w_out, norm_final):
    b_p, seq, d = x_prompt.shape
    b_s, n_new, _ = x_sample.shape
    assert norm_in.shape[0] == 1, "one layer: the final norm is fused into the mixer call"
    assert n_new == CHUNK and seq % 256 == 0 and seq >= HIST_A
    assert cache_a_k.shape[2] == HIST_A and cache_b_k.shape[2] == HIST_B

    w_in_b = w_in[0].astype(BF16)
    w_out_b = w_out[0].astype(BF16)
    g_in = norm_in[0].reshape(1, d)
    g_fin = norm_final.reshape(1, d)
    sinks = sinks_b[0]

    z_p = _inproj(x_prompt.reshape(b_p * seq, d), g_in, w_in_b, BF16, tm=512)
    y_p, ak_p, av_p, bk_p, bv_p = _mixer(
        z_p, x_prompt, None, jnp.swapaxes(_rel_bias(rel_bias_a[0], 128), 1, 2), _alibi_bias(128), sinks, w_out_b, g_fin,
        tq=256, group=128, keep_a=HIST_A, keep_b=HIST_B)

    z_s = _inproj(x_sample.reshape(b_s * n_new, d), g_in, w_in_b, F32, tm=b_s * n_new)
    caches = (cache_a_k[0].reshape(b_s, HIST_A, D_A), cache_a_v[0].reshape(b_s, HIST_A, D_A),
              cache_b_k[0].reshape(b_s, HIST_B, KV_B), cache_b_v[0].reshape(b_s, HIST_B, KV_B))
    y_s, ak_s, av_s, bk_s, bv_s = _mixer(
        z_s, x_sample, caches, jnp.swapaxes(_rel_bias(rel_bias_a[0], CHUNK), 1, 2), _alibi_bias(CHUNK), sinks, w_out_b, g_fin,
        tq=CHUNK, group=CHUNK, keep_a=n_new, keep_b=n_new)

    def heads(rows, n_heads):
        return rows.reshape(1, rows.shape[0], rows.shape[1], n_heads, HEAD_DIM)

    def rolled(cache, new, n_heads):
        return jnp.concatenate([cache, heads(new, n_heads)], axis=2)[:, :, n_new:]

    return (y_p, y_s,
            heads(ak_p, N_HEADS), heads(av_p, N_HEADS), heads(bk_p, N_KV_B), heads(bv_p, N_KV_B),
            rolled(cache_a_k, ak_s, N_HEADS), rolled(cache_a_v, av_s, N_HEADS),
            rolled(cache_b_k, bk_s, N_KV_B), rolled(cache_b_v, bv_s, N_KV_B))
```

```python
import functools

import jax
import jax.numpy as jnp
import numpy as np
from jax import lax
from jax.experimental import pallas as pl
from jax.experimental.pallas import tpu as pltpu

CHUNK = 64
HEAD_DIM = 64
N_HEADS = 16
N_KV_B = 4
GROUP_B = N_HEADS // N_KV_B
D_A = N_HEADS * HEAD_DIM
D_B = N_HEADS * HEAD_DIM
KV_B = N_KV_B * HEAD_DIM
HIST_A = 8 * CHUNK
HIST_B = 2 * CHUNK
REL_CLIP = 256
N_REL = REL_CLIP + CHUNK
RMS_EPS = 1e-6
NEG_INF = -1e30
LOG2E = 1.4426950408889634

LANES = 128
N_PAIRS = D_A // LANES
QA_C, KA_C, VA_C, GA_C = 0, N_PAIRS, 2 * N_PAIRS, 3 * N_PAIRS
QB_C = 4 * N_PAIRS
KB_C = QB_C + N_PAIRS
VB_C = KB_C + KV_B // LANES
GB_C = VB_C + KV_B // LANES
N_COLS = GB_C + N_PAIRS
D_IN = N_COLS * LANES

VMEM_LIMIT_BYTES = 56 * 1024 * 1024

BF16 = jnp.bfloat16
F32 = jnp.float32


def _resident(shape):
    return pl.BlockSpec(shape, lambda *_: (0,) * len(shape), pipeline_mode=pl.Buffered(1))


def _rel_bias_kernel(tab_ref, out_ref, *, group, width):
    band = out_ref.shape[2]
    n = lax.broadcasted_iota(jnp.int32, (N_REL, width), 0)
    u = lax.broadcasted_iota(jnp.int32, (N_REL, width), 1)
    idx = jnp.clip(HIST_A + group - 1 - u, -(CHUNK - 1), REL_CLIP) + (CHUNK - 1)
    onehot = (n == idx).astype(F32)
    gv = jnp.dot(tab_ref[...], onehot, precision=lax.Precision.HIGHEST, preferred_element_type=F32) * LOG2E
    r = lax.broadcasted_iota(jnp.int32, (group, band), 0)
    j = lax.broadcasted_iota(jnp.int32, (group, band), 1)
    rel = j - (r // CHUNK) * CHUNK
    valid = (rel >= 0) & (rel < HIST_A + CHUNK)
    for h in range(N_HEADS):
        rows = jnp.broadcast_to(gv[h:h + 1, :], (group, width))
        rows = pltpu.roll(rows, width - group + 1, 1, stride=1, stride_axis=0)
        out_ref[h // 2, (h % 2) * group:(h % 2 + 1) * group, :] = jnp.where(valid, rows[:, :band], NEG_INF)


def _rel_bias(table, group):
    band = HIST_A + group
    width = -(-(HIST_A + 2 * group - 1) // LANES) * LANES
    return pl.pallas_call(
        functools.partial(_rel_bias_kernel, group=group, width=width),
        out_shape=jax.ShapeDtypeStruct((N_PAIRS, 2 * group, band), F32),
        name=f"rel_bias_g{group}",
    )(table)


def _alibi_bias(group):
    r = np.arange(group)[:, None]
    j = np.arange(HIST_B + group)[None, :]
    d = np.abs(HIST_B + r - j).astype(np.float32)
    slopes = (2.0 ** (-8.0 * np.arange(1, N_HEADS + 1, dtype=np.float32) / N_HEADS)).astype(np.float32)
    bias = (-slopes[:, None, None] * d[None]).astype(np.float64) * LOG2E
    rel = j - (r // CHUNK) * CHUNK
    valid = (rel >= 0) & (rel < HIST_B + CHUNK)
    bias = np.where(valid[None], bias, np.float32(NEG_INF)).astype(np.float32)
    return jnp.asarray(bias.reshape(N_KV_B, GROUP_B * group, HIST_B + group))


INPROJ_TN = 4 * LANES
Q_CHUNKS = tuple(c for c0 in (QA_C, QB_C) for c in range(c0, c0 + N_PAIRS, INPROJ_TN // LANES))


def _inproj_kernel(x_ref, g_ref, w_ref, z_ref, h_ref, *, tn):
    x = x_ref[...]
    ms = jnp.mean(x * x, axis=-1, keepdims=True)
    h_ref[...] = (x * lax.rsqrt(ms + RMS_EPS) * g_ref[...]).astype(h_ref.dtype)
    for n0 in range(0, w_ref.shape[1], tn):
        res = jnp.dot(h_ref[...], w_ref[:, n0:n0 + tn], preferred_element_type=F32)
        c0 = n0 // LANES
        if c0 in Q_CHUNKS:
            res = res * (HEAD_DIM ** -0.5 * LOG2E)
        res = res.astype(z_ref.dtype)
        for j in range(tn // LANES):
            z_ref[c0 + j] = res[:, j * LANES:(j + 1) * LANES]


def _inproj(x2d, gain, w_bf16, out_dtype, tm):
    m, d = x2d.shape
    return pl.pallas_call(
        functools.partial(_inproj_kernel, tn=INPROJ_TN),
        grid=(m // tm,),
        in_specs=[
            pl.BlockSpec((tm, d), lambda i: (i, 0)),
            _resident((1, d)),
            _resident((d, D_IN)),
        ],
        out_specs=pl.BlockSpec((N_COLS, tm, LANES), lambda i: (0, i, 0)),
        out_shape=jax.ShapeDtypeStruct((N_COLS, m, LANES), out_dtype),
        scratch_shapes=[pltpu.VMEM((tm, d), BF16)],
        compiler_params=pltpu.CompilerParams(
            dimension_semantics=("arbitrary",), vmem_limit_bytes=VMEM_LIMIT_BYTES),
        name=f"inproj_m{m}",
    )(x2d, gain, w_bf16)


def _dup_halves(x):
    lane = lax.broadcasted_iota(jnp.int32, x.shape, 1)
    rolled = pltpu.roll(x, HEAD_DIM, 1)
    return (jnp.where(lane < HEAD_DIM, x, rolled).astype(BF16),
            jnp.where(lane < HEAD_DIM, rolled, x).astype(BF16))


def _split_heads(q):
    lane = lax.broadcasted_iota(jnp.int32, q.shape, 1)
    zero = jnp.zeros_like(q)
    return [jnp.where(lane < HEAD_DIM, q, zero), jnp.where(lane >= HEAD_DIM, q, zero)]


N_SLOTS = 2


def _pipeline(n_blocks, stage1, stage2, stage3):
    for b in range(n_blocks + 2):
        if b < n_blocks:
            stage1(b, b % N_SLOTS)
        if 0 <= b - 1 < n_blocks:
            stage2(b - 1, (b - 1) % N_SLOTS)
        if 0 <= b - 2 < n_blocks:
            stage3(b - 2, (b - 2) % N_SLOTS)


def _mixer_kernel(*refs, tq, group, n_hist_a, n_hist_b, has_cache, n_tiles, keep_a, keep_b):
    if has_cache:
        z_ref, x_ref, cak_ref, cav_ref, cbk_ref, cbv_ref = refs[:6]
        refs = refs[6:]
    else:
        z_ref, x_ref = refs[:2]
        refs = refs[2:]
    (bias_a_ref, bias_b_ref, sinks_ref, wout_ref, gfin_ref,
     y_ref, tak_ref, tav_ref, tbk_ref, tbv_ref,
     kat_s, va_s, kbt_s, vb_s, sa_s, pa_s, ma_s, la_s, sb_s, pb_s, mb_s, lb_s, o_s, y_s) = refs
    t = pl.program_id(1)
    band_a = HIST_A + group
    band_b = HIST_B + group
    n_groups = tq // group

    def transposed(k):
        return k.astype(F32).T.astype(BF16)

    def init_history():
        for p in range(N_PAIRS):
            if has_cache:
                kat_s[p, :, 0:HIST_A] = transposed(cak_ref[0, :, p * LANES:(p + 1) * LANES])
                va_s[p, 0:HIST_A, :] = cav_ref[0, :, p * LANES:(p + 1) * LANES].astype(BF16)
            else:
                kat_s[p, :, 0:HIST_A] = jnp.zeros((LANES, HIST_A), BF16)
                va_s[p, 0:HIST_A, :] = jnp.zeros((HIST_A, LANES), BF16)
        for blk in range(KV_B // LANES):
            if has_cache:
                k_lo, k_hi = _dup_halves(cbk_ref[0, :, blk * LANES:(blk + 1) * LANES])
                v_lo, v_hi = _dup_halves(cbv_ref[0, :, blk * LANES:(blk + 1) * LANES])
            else:
                k_lo = k_hi = v_lo = v_hi = jnp.zeros((HIST_B, LANES), BF16)
            kbt_s[2 * blk, :, 0:HIST_B] = transposed(k_lo)
            kbt_s[2 * blk + 1, :, 0:HIST_B] = transposed(k_hi)
            vb_s[2 * blk, 0:HIST_B, :] = v_lo
            vb_s[2 * blk + 1, 0:HIST_B, :] = v_hi

    def write_tail(ref, c0, n_cblk, keep):
        def copy(rows):
            for p in range(n_cblk):
                ref[0, :, p * LANES:(p + 1) * LANES] = z_ref[c0 + p, rows, :].astype(F32)
        if keep >= tq:
            pl.when(t >= n_tiles - keep // tq)(lambda: copy(slice(None)))
        else:
            pl.when(t == n_tiles - 1)(lambda: copy(slice(tq - keep, tq)))

    lane = lax.broadcasted_iota(jnp.int32, (group, LANES), 1)

    def lanes_to(x, width):
        return jnp.tile(x, (1, -(-width // LANES)))[:, :width]

    def stat(x):
        return jnp.broadcast_to(x, (x.shape[0], LANES))

    def attend(masked):
        for sg in range(n_groups):
            attend_group(sg, masked)

    def attend_group(sg, masked):
        rows = slice(sg * group, (sg + 1) * group)
        rows_a = slice(sg * group, sg * group + band_a)
        rows_b = slice(sg * group, sg * group + band_b)

        def load_q(c):
            return z_ref[c, rows, :].astype(BF16)

        def store_gated(c_out, c_gate, o):
            g = z_ref[c_gate, rows, :].astype(F32)
            o_s[rows, c_out * LANES:(c_out + 1) * LANES] = (o * (g * jax.nn.sigmoid(g))).astype(BF16)

        def mask_history(s, hist, n_hist):
            if not masked:
                return s
            col = lax.broadcasted_iota(jnp.int32, s.shape, 1)
            return jnp.where(col >= hist - n_hist - (t * tq + sg * group), s, NEG_INF)

        def a_scores(p, slot):
            q2 = jnp.concatenate(_split_heads(load_q(QA_C + p)), axis=0)
            s = jnp.dot(q2, kat_s[p, :, rows_a], preferred_element_type=F32) + bias_a_ref[p]
            s = mask_history(s, HIST_A, n_hist_a)
            sa_s[slot] = s
            ma_s[slot] = stat(jnp.max(s, axis=-1, keepdims=True))

        def a_softmax(p, slot):
            e = jnp.exp2(sa_s[slot] - lanes_to(ma_s[slot], band_a))
            la_s[slot] = stat(jnp.sum(e, axis=-1, keepdims=True))
            pa_s[slot] = e.astype(BF16)

        def a_values(p, slot):
            pv = jnp.dot(pa_s[slot], va_s[p, rows_a, :], preferred_element_type=F32) / la_s[slot]
            store_gated(p, GA_C + p, jnp.where(lane < HEAD_DIM, pv[:group], pv[group:]))

        def sink_rows(g):
            return jnp.concatenate(
                [jnp.full((group, LANES), sinks_ref[GROUP_B * g + i] * LOG2E, F32) for i in range(GROUP_B)],
                axis=0)

        def b_scores(g, slot):
            q4 = []
            for lb in range(2):
                q4 += _split_heads(load_q(QB_C + 2 * g + lb))
            s = jnp.dot(jnp.concatenate(q4, axis=0), kbt_s[g, :, rows_b],
                        preferred_element_type=F32) + bias_b_ref[g]
            s = mask_history(s, HIST_B, n_hist_b)
            sb_s[slot] = s
            mb_s[slot] = jnp.maximum(stat(jnp.max(s, axis=-1, keepdims=True)), sink_rows(g))

        def b_softmax(g, slot):
            m = mb_s[slot]
            e = jnp.exp2(sb_s[slot] - lanes_to(m, band_b))
            lb_s[slot] = stat(jnp.sum(e, axis=-1, keepdims=True)) + jnp.exp2(sink_rows(g) - m)
            pb_s[slot] = e.astype(BF16)

        def b_values(g, slot):
            pv = jnp.dot(pb_s[slot], vb_s[g, rows_b, :], preferred_element_type=F32) / lb_s[slot]
            for lb in range(2):
                top = pv[2 * lb * group:(2 * lb + 1) * group]
                bot = pv[(2 * lb + 1) * group:(2 * lb + 2) * group]
                c = 2 * g + lb
                store_gated(N_PAIRS + c, GB_C + c, jnp.where(lane < HEAD_DIM, top, bot))

        _pipeline(N_PAIRS, a_scores, a_softmax, a_values)
        _pipeline(N_KV_B, b_scores, b_softmax, b_values)

        d_model = y_s.shape[1]
        tn = 512
        for n0 in range(0, d_model, tn):
            y_s[rows, n0:n0 + tn] = x_ref[0, rows, n0:n0 + tn] + jnp.dot(
                o_s[rows, :], wout_ref[:, n0:n0 + tn], preferred_element_type=F32)
        y = y_s[rows, :]
        ms = jnp.mean(y * y, axis=-1, keepdims=True)
        y_ref[0, rows, :] = y * lax.rsqrt(ms + RMS_EPS) * gfin_ref[...]

    pl.when(t == 0)(init_history)
    for p in range(N_PAIRS):
        kat_s[p, :, HIST_A:HIST_A + tq] = transposed(z_ref[KA_C + p])
        va_s[p, HIST_A:HIST_A + tq, :] = z_ref[VA_C + p].astype(BF16)
    for blk in range(KV_B // LANES):
        k_lo, k_hi = _dup_halves(z_ref[KB_C + blk].astype(F32))
        v_lo, v_hi = _dup_halves(z_ref[VB_C + blk].astype(F32))
        kbt_s[2 * blk, :, HIST_B:HIST_B + tq] = transposed(k_lo)
        kbt_s[2 * blk + 1, :, HIST_B:HIST_B + tq] = transposed(k_hi)
        vb_s[2 * blk, HIST_B:HIST_B + tq, :] = v_lo
        vb_s[2 * blk + 1, HIST_B:HIST_B + tq, :] = v_hi
    write_tail(tak_ref, KA_C, N_PAIRS, keep_a)
    write_tail(tav_ref, VA_C, N_PAIRS, keep_a)
    write_tail(tbk_ref, KB_C, KV_B // LANES, keep_b)
    write_tail(tbv_ref, VB_C, KV_B // LANES, keep_b)

    if n_hist_a < HIST_A or n_hist_b < HIST_B:
        pl.when(t * tq < HIST_A)(lambda: attend(True))
        pl.when(t * tq >= HIST_A)(lambda: attend(False))
    else:
        attend(False)

    if n_tiles > 1:
        for ktbuf, vbuf, hist in ((kat_s, va_s, HIST_A), (kbt_s, vb_s, HIST_B)):
            step = min(tq, hist)
            for d0 in range(0, hist, step):
                ktbuf[:, :, d0:d0 + step] = ktbuf[:, :, d0 + tq:d0 + tq + step]
                vbuf[:, d0:d0 + step, :] = vbuf[:, d0 + tq:d0 + tq + step, :]


def _mixer(z3, x, caches, bias_a, bias_b, sinks, wout_bf16, gfin, tq, group, keep_a, keep_b):
    nb, s, d = x.shape
    n_tiles = s // tq
    has_cache = caches is not None
    band_a = HIST_A + group
    band_b = HIST_B + group

    def tail_spec(keep, width):
        if keep >= tq:
            nblk = keep // tq
            return pl.BlockSpec((1, tq, width), lambda b, t: (b, jnp.clip(t - (n_tiles - nblk), 0, nblk - 1), 0))
        return pl.BlockSpec((1, keep, width), lambda b, t: (b, 0, 0))

    in_specs = [pl.BlockSpec((N_COLS, tq, LANES), lambda b, t: (0, b * n_tiles + t, 0)),
                pl.BlockSpec((1, tq, d), lambda b, t: (b, t, 0))]
    args = [z3, x]
    if has_cache:
        for c in caches:
            in_specs.append(pl.BlockSpec((1,) + c.shape[1:], lambda b, t: (b, 0, 0)))
            args.append(c)
    in_specs += [
        _resident(bias_a.shape),
        _resident(bias_b.shape),
        pl.BlockSpec(memory_space=pltpu.SMEM),
        _resident(wout_bf16.shape),
        _resident(gfin.shape),
    ]
    args += [bias_a, bias_b, sinks, wout_bf16, gfin]
    kern = functools.partial(
        _mixer_kernel, tq=tq, group=group,
        n_hist_a=HIST_A if has_cache else 0, n_hist_b=HIST_B if has_cache else 0,
        has_cache=has_cache, n_tiles=n_tiles, keep_a=keep_a, keep_b=keep_b)
    return pl.pallas_call(
        kern,
        grid=(nb, n_tiles),
        in_specs=in_specs,
        out_specs=[
            pl.BlockSpec((1, tq, d), lambda b, t: (b, t, 0)),
            tail_spec(keep_a, D_A), tail_spec(keep_a, D_A),
            tail_spec(keep_b, KV_B), tail_spec(keep_b, KV_B),
        ],
        out_shape=[
            jax.ShapeDtypeStruct((nb, s, d), F32),
            jax.ShapeDtypeStruct((nb, keep_a, D_A), F32), jax.ShapeDtypeStruct((nb, keep_a, D_A), F32),
            jax.ShapeDtypeStruct((nb, keep_b, KV_B), F32), jax.ShapeDtypeStruct((nb, keep_b, KV_B), F32),
        ],
        scratch_shapes=[
            pltpu.VMEM((N_PAIRS, LANES, HIST_A + tq), BF16),
            pltpu.VMEM((N_PAIRS, HIST_A + tq, LANES), BF16),
            pltpu.VMEM((N_KV_B, LANES, HIST_B + tq), BF16),
            pltpu.VMEM((N_KV_B, HIST_B + tq, LANES), BF16),
            pltpu.VMEM((N_SLOTS, 2 * group, band_a), F32),
            pltpu.VMEM((N_SLOTS, 2 * group, band_a), BF16),
            pltpu.VMEM((N_SLOTS, 2 * group, LANES), F32),
            pltpu.VMEM((N_SLOTS, 2 * group, LANES), F32),
            pltpu.VMEM((N_SLOTS, GROUP_B * group, band_b), F32),
            pltpu.VMEM((N_SLOTS, GROUP_B * group, band_b), BF16),
            pltpu.VMEM((N_SLOTS, GROUP_B * group, LANES), F32),
            pltpu.VMEM((N_SLOTS, GROUP_B * group, LANES), F32),
            pltpu.VMEM((tq, D_A + D_B), BF16),
            pltpu.VMEM((tq, d), F32),
        ],
        compiler_params=pltpu.CompilerParams(
            dimension_semantics=("arbitrary", "arbitrary"), vmem_limit_bytes=VMEM_LIMIT_BYTES),
        name=f"mixer_tq{tq}",
    )(*args)


def kernel(x_prompt, x_sample, cache_a_k, cache_a_v, cache_b_k, cache_b_v, norm_in, w_in, rel_bias_a,
           sinks_b, w_out, norm_final):
    b_p, seq, d = x_prompt.shape
    b_s, n_new, _ = x_sample.shape
    assert norm_in.shape[0] == 1, "one layer: the final norm is fused into the mixer call"
    assert n_new == CHUNK and seq % 256 == 0 and seq >= HIST_A
    assert cache_a_k.shape[2] == HIST_A and cache_b_k.shape[2] == HIST_B

    w_in_b = w_in[0].astype(BF16)
    w_out_b = w_out[0].astype(BF16)
    g_in = norm_in[0].reshape(1, d)
    g_fin = norm_final.reshape(1, d)
    sinks = sinks_b[0]

    z_p = _inproj(x_prompt.reshape(b_p * seq, d), g_in, w_in_b, BF16, tm=512)
    y_p, ak_p, av_p, bk_p, bv_p = _mixer(
        z_p, x_prompt, None, _rel_bias(rel_bias_a[0], 128), _alibi_bias(128), sinks, w_out_b, g_fin,
        tq=256, group=128, keep_a=HIST_A, keep_b=HIST_B)

    z_s = _inproj(x_sample.reshape(b_s * n_new, d), g_in, w_in_b, F32, tm=b_s * n_new)
    caches = (cache_a_k[0].reshape(b_s, HIST_A, D_A), cache_a_v[0].reshape(b_s, HIST_A, D_A),
              cache_b_k[0].reshape(b_s, HIST_B, KV_B), cache_b_v[0].reshape(b_s, HIST_B, KV_B))
    y_s, ak_s, av_s, bk_s, bv_s = _mixer(
        z_s, x_sample, caches, _rel_bias(rel_bias_a[0], CHUNK), _alibi_bias(CHUNK), sinks, w_out_b, g_fin,
        tq=CHUNK, group=CHUNK, keep_a=n_new, keep_b=n_new)

    def heads(rows, n_heads):
        return rows.reshape(1, rows.shape[0], rows.shape[1], n_heads, HEAD_DIM)

    def rolled(cache, new, n_heads):
        return jnp.concatenate([cache, heads(new, n_heads)], axis=2)[:, :, n_new:]

    return (y_p, y_s,
            heads(ak_p, N_HEADS), heads(av_p, N_HEADS), heads(bk_p, N_KV_B), heads(bv_p, N_KV_B),
            rolled(cache_a_k, ak_s, N_HEADS), rolled(cache_a_v, av_s, N_HEADS),
            rolled(cache_b_k, bk_s, N_KV_B), rolled(cache_b_v, bv_s, N_KV_B))
```

```python
import functools

import jax
import jax.numpy as jnp
import numpy as np
from jax import lax
from jax.experimental import pallas as pl
from jax.experimental.pallas import tpu as pltpu

CHUNK = 64
HEAD_DIM = 64
N_HEADS = 16
N_KV_B = 4
GROUP_B = N_HEADS // N_KV_B
D_A = N_HEADS * HEAD_DIM
D_B = N_HEADS * HEAD_DIM
KV_B = N_KV_B * HEAD_DIM
HIST_A = 8 * CHUNK
HIST_B = 2 * CHUNK
REL_CLIP = 256
N_REL = REL_CLIP + CHUNK
RMS_EPS = 1e-6
NEG_INF = -1e30
LOG2E = 1.4426950408889634

LANES = 128
N_PAIRS = D_A // LANES
QA_C, KA_C, VA_C, GA_C = 0, N_PAIRS, 2 * N_PAIRS, 3 * N_PAIRS
QB_C = 4 * N_PAIRS
KB_C = QB_C + N_PAIRS
VB_C = KB_C + KV_B // LANES
GB_C = VB_C + KV_B // LANES
N_COLS = GB_C + N_PAIRS
D_IN = N_COLS * LANES

VMEM_LIMIT_BYTES = 60000 * 1024

BF16 = jnp.bfloat16
F32 = jnp.float32
_NT = (((1,), (1,)), ((), ()))


def _resident(shape):
    return pl.BlockSpec(shape, lambda *_: (0,) * len(shape), pipeline_mode=pl.Buffered(1))


def _rel_bias_kernel(tab_ref, out_ref, *, group, width):
    band = out_ref.shape[2]
    n = lax.broadcasted_iota(jnp.int32, (N_REL, width), 0)
    u = lax.broadcasted_iota(jnp.int32, (N_REL, width), 1)
    idx = jnp.clip(HIST_A + group - 1 - u, -(CHUNK - 1), REL_CLIP) + (CHUNK - 1)
    onehot = (n == idx).astype(F32)
    gv = jnp.dot(tab_ref[...], onehot, precision=lax.Precision.HIGHEST, preferred_element_type=F32) * LOG2E
    r = lax.broadcasted_iota(jnp.int32, (group, band), 0)
    j = lax.broadcasted_iota(jnp.int32, (group, band), 1)
    rel = j - (r // CHUNK) * CHUNK
    valid = (rel >= 0) & (rel < HIST_A + CHUNK)
    for h in range(N_HEADS):
        rows = jnp.broadcast_to(gv[h:h + 1, :], (group, width))
        rows = pltpu.roll(rows, width - group + 1, 1, stride=1, stride_axis=0)
        out_ref[h // 2, (h % 2) * group:(h % 2 + 1) * group, :] = jnp.where(valid, rows[:, :band], NEG_INF)


def _rel_bias(table, group):
    band = HIST_A + group
    width = -(-(HIST_A + 2 * group - 1) // LANES) * LANES
    return pl.pallas_call(
        functools.partial(_rel_bias_kernel, group=group, width=width),
        out_shape=jax.ShapeDtypeStruct((N_PAIRS, 2 * group, band), F32),
        name=f"rel_bias_g{group}",
    )(table)


def _alibi_bias(group):
    r = np.arange(group)[:, None]
    j = np.arange(HIST_B + group)[None, :]
    d = np.abs(HIST_B + r - j).astype(np.float32)
    slopes = (2.0 ** (-8.0 * np.arange(1, N_HEADS + 1, dtype=np.float32) / N_HEADS)).astype(np.float32)
    bias = (-slopes[:, None, None] * d[None]).astype(np.float64) * LOG2E
    rel = j - (r // CHUNK) * CHUNK
    valid = (rel >= 0) & (rel < HIST_B + CHUNK)
    bias = np.where(valid[None], bias, np.float32(NEG_INF)).astype(np.float32)
    return jnp.asarray(bias.reshape(N_KV_B, GROUP_B * group, HIST_B + group))


INPROJ_TN = 4 * LANES
Q_CHUNKS = tuple(c for c0 in (QA_C, QB_C) for c in range(c0, c0 + N_PAIRS, INPROJ_TN // LANES))


def _inproj_kernel(x_ref, g_ref, w_ref, z_ref, h_ref, *, tn):
    x = x_ref[...]
    ms = jnp.mean(x * x, axis=-1, keepdims=True)
    h_ref[...] = (x * lax.rsqrt(ms + RMS_EPS) * g_ref[...]).astype(h_ref.dtype)
    for n0 in range(0, w_ref.shape[1], tn):
        res = jnp.dot(h_ref[...], w_ref[:, n0:n0 + tn], preferred_element_type=F32)
        c0 = n0 // LANES
        if c0 in Q_CHUNKS:
            res = res * (HEAD_DIM ** -0.5 * LOG2E)
        res = res.astype(z_ref.dtype)
        for j in range(tn // LANES):
            z_ref[c0 + j] = res[:, j * LANES:(j + 1) * LANES]


def _inproj(x2d, gain, w_bf16, out_dtype, tm):
    m, d = x2d.shape
    return pl.pallas_call(
        functools.partial(_inproj_kernel, tn=INPROJ_TN),
        grid=(m // tm,),
        in_specs=[
            pl.BlockSpec((tm, d), lambda i: (i, 0)),
            _resident((1, d)),
            _resident((d, D_IN)),
        ],
        out_specs=pl.BlockSpec((N_COLS, tm, LANES), lambda i: (0, i, 0)),
        out_shape=jax.ShapeDtypeStruct((N_COLS, m, LANES), out_dtype),
        scratch_shapes=[pltpu.VMEM((tm, d), BF16)],
        compiler_params=pltpu.CompilerParams(
            dimension_semantics=("arbitrary",), vmem_limit_bytes=VMEM_LIMIT_BYTES),
        name=f"inproj_m{m}",
    )(x2d, gain, w_bf16)


def _dup_halves(x):
    lane = lax.broadcasted_iota(jnp.int32, x.shape, 1)
    rolled = pltpu.roll(x, HEAD_DIM, 1)
    return (jnp.where(lane < HEAD_DIM, x, rolled).astype(BF16),
            jnp.where(lane < HEAD_DIM, rolled, x).astype(BF16))


def _split_heads(q):
    lane = lax.broadcasted_iota(jnp.int32, q.shape, 1)
    zero = jnp.zeros_like(q)
    return [jnp.where(lane < HEAD_DIM, q, zero), jnp.where(lane >= HEAD_DIM, q, zero)]


N_SLOTS = 2


def _pipeline(n_blocks, stage1, stage2, stage3):
    for b in range(n_blocks + 2):
        if b < n_blocks:
            stage1(b, b % N_SLOTS)
        if 0 <= b - 1 < n_blocks:
            stage2(b - 1, (b - 1) % N_SLOTS)
        if 0 <= b - 2 < n_blocks:
            stage3(b - 2, (b - 2) % N_SLOTS)


def _mixer_kernel(*refs, tq, group, n_hist_a, n_hist_b, has_cache, n_tiles, keep_a, keep_b):
    if has_cache:
        z_ref, x_ref, cak_ref, cav_ref, cbk_ref, cbv_ref = refs[:6]
        refs = refs[6:]
    else:
        z_ref, x_ref = refs[:2]
        refs = refs[2:]
    (bias_a_ref, bias_b_ref, sinks_ref, wout_ref, gfin_ref,
     y_ref, tak_ref, tav_ref, tbk_ref, tbv_ref,
     ka_s, va_s, kb_s, vb_s, sa_s, pa_s, ma_s, la_s, sb_s, pb_s, mb_s, lb_s, o_s, y_s) = refs
    t = pl.program_id(1)
    band_a = HIST_A + group
    band_b = HIST_B + group
    n_groups = tq // group

    def init_history():
        for p in range(N_PAIRS):
            if has_cache:
                ka_s[p, 0:HIST_A, :] = cak_ref[0, :, p * LANES:(p + 1) * LANES].astype(BF16)
                va_s[p, 0:HIST_A, :] = cav_ref[0, :, p * LANES:(p + 1) * LANES].astype(BF16)
            else:
                ka_s[p, 0:HIST_A, :] = jnp.zeros((HIST_A, LANES), BF16)
                va_s[p, 0:HIST_A, :] = jnp.zeros((HIST_A, LANES), BF16)
        for blk in range(KV_B // LANES):
            for buf, cache in ((kb_s, cbk_ref if has_cache else None), (vb_s, cbv_ref if has_cache else None)):
                if has_cache:
                    lo, hi = _dup_halves(cache[0, :, blk * LANES:(blk + 1) * LANES])
                else:
                    lo = hi = jnp.zeros((HIST_B, LANES), BF16)
                buf[2 * blk, 0:HIST_B, :] = lo
                buf[2 * blk + 1, 0:HIST_B, :] = hi

    def write_tail(ref, c0, n_cblk, keep):
        def copy(rows):
            for p in range(n_cblk):
                ref[0, :, p * LANES:(p + 1) * LANES] = z_ref[c0 + p, rows, :].astype(F32)
        if keep >= tq:
            pl.when(t >= n_tiles - keep // tq)(lambda: copy(slice(None)))
        else:
            pl.when(t == n_tiles - 1)(lambda: copy(slice(tq - keep, tq)))

    lane = lax.broadcasted_iota(jnp.int32, (group, LANES), 1)

    def lanes_to(x, width):
        return jnp.tile(x, (1, -(-width // LANES)))[:, :width]

    def stat(x):
        return jnp.broadcast_to(x, (x.shape[0], LANES))

    def attend(masked):
        for sg in range(n_groups):
            attend_group(sg, masked)
        d_model = y_s.shape[1]
        tn = 512
        for n0 in range(0, d_model, tn):
            y_s[:, n0:n0 + tn] = x_ref[0, :, n0:n0 + tn] + jnp.dot(
                o_s[...], wout_ref[:, n0:n0 + tn], preferred_element_type=F32)
        y = y_s[...]
        ms = jnp.mean(y * y, axis=-1, keepdims=True)
        y_ref[0] = y * lax.rsqrt(ms + RMS_EPS) * gfin_ref[...]

    def attend_group(sg, masked):
        rows = slice(sg * group, (sg + 1) * group)
        rows_a = slice(sg * group, sg * group + band_a)
        rows_b = slice(sg * group, sg * group + band_b)

        def load_q(c):
            return z_ref[c, rows, :].astype(BF16)

        def store_gated(c_out, c_gate, o):
            g = z_ref[c_gate, rows, :].astype(F32)
            o_s[rows, c_out * LANES:(c_out + 1) * LANES] = (o * (g * jax.nn.sigmoid(g))).astype(BF16)

        def mask_history(s, hist, n_hist):
            if not masked:
                return s
            col = lax.broadcasted_iota(jnp.int32, s.shape, 1)
            return jnp.where(col >= hist - n_hist - (t * tq + sg * group), s, NEG_INF)

        def a_scores(p, slot):
            q2 = jnp.concatenate(_split_heads(load_q(QA_C + p)), axis=0)
            s = lax.dot_general(q2, ka_s[p, rows_a, :], _NT, preferred_element_type=F32) + bias_a_ref[p]
            s = mask_history(s, HIST_A, n_hist_a)
            sa_s[slot] = s
            ma_s[slot] = stat(jnp.max(s, axis=-1, keepdims=True))

        def a_softmax(p, slot):
            e = jnp.exp2(sa_s[slot] - lanes_to(ma_s[slot], band_a))
            la_s[slot] = stat(jnp.sum(e, axis=-1, keepdims=True))
            pa_s[slot] = e.astype(BF16)

        def a_values(p, slot):
            pv = jnp.dot(pa_s[slot], va_s[p, rows_a, :], preferred_element_type=F32) / la_s[slot]
            store_gated(p, GA_C + p, jnp.where(lane < HEAD_DIM, pv[:group], pv[group:]))

        def sink_rows(g):
            return jnp.concatenate(
                [jnp.full((group, LANES), sinks_ref[GROUP_B * g + i] * LOG2E, F32) for i in range(GROUP_B)],
                axis=0)

        def b_scores(g, slot):
            q4 = []
            for lb in range(2):
                q4 += _split_heads(load_q(QB_C + 2 * g + lb))
            s = lax.dot_general(jnp.concatenate(q4, axis=0), kb_s[g, rows_b, :], _NT,
                                preferred_element_type=F32) + bias_b_ref[g]
            s = mask_history(s, HIST_B, n_hist_b)
            sb_s[slot] = s
            mb_s[slot] = jnp.maximum(stat(jnp.max(s, axis=-1, keepdims=True)), sink_rows(g))

        def b_softmax(g, slot):
            m = mb_s[slot]
            e = jnp.exp2(sb_s[slot] - lanes_to(m, band_b))
            lb_s[slot] = stat(jnp.sum(e, axis=-1, keepdims=True)) + jnp.exp2(sink_rows(g) - m)
            pb_s[slot] = e.astype(BF16)

        def b_values(g, slot):
            pv = jnp.dot(pb_s[slot], vb_s[g, rows_b, :], preferred_element_type=F32) / lb_s[slot]
            for lb in range(2):
                top = pv[2 * lb * group:(2 * lb + 1) * group]
                bot = pv[(2 * lb + 1) * group:(2 * lb + 2) * group]
                c = 2 * g + lb
                store_gated(N_PAIRS + c, GB_C + c, jnp.where(lane < HEAD_DIM, top, bot))

        _pipeline(N_PAIRS, a_scores, a_softmax, a_values)
        _pipeline(N_KV_B, b_scores, b_softmax, b_values)

    pl.when(t == 0)(init_history)
    for p in range(N_PAIRS):
        ka_s[p, HIST_A:HIST_A + tq, :] = z_ref[KA_C + p].astype(BF16)
        va_s[p, HIST_A:HIST_A + tq, :] = z_ref[VA_C + p].astype(BF16)
    for blk in range(KV_B // LANES):
        for buf, c0 in ((kb_s, KB_C), (vb_s, VB_C)):
            lo, hi = _dup_halves(z_ref[c0 + blk].astype(F32))
            buf[2 * blk, HIST_B:HIST_B + tq, :] = lo
            buf[2 * blk + 1, HIST_B:HIST_B + tq, :] = hi
    write_tail(tak_ref, KA_C, N_PAIRS, keep_a)
    write_tail(tav_ref, VA_C, N_PAIRS, keep_a)
    write_tail(tbk_ref, KB_C, KV_B // LANES, keep_b)
    write_tail(tbv_ref, VB_C, KV_B // LANES, keep_b)

    if n_hist_a < HIST_A or n_hist_b < HIST_B:
        pl.when(t * tq < HIST_A)(lambda: attend(True))
        pl.when(t * tq >= HIST_A)(lambda: attend(False))
    else:
        attend(False)

    if n_tiles > 1:
        for buf, hist in ((ka_s, HIST_A), (va_s, HIST_A), (kb_s, HIST_B), (vb_s, HIST_B)):
            step = min(tq, hist)
            for d0 in range(0, hist, step):
                buf[:, d0:d0 + step, :] = buf[:, d0 + tq:d0 + tq + step, :]


def _mixer(z3, x, caches, bias_a, bias_b, sinks, wout_bf16, gfin, tq, group, keep_a, keep_b):
    nb, s, d = x.shape
    n_tiles = s // tq
    has_cache = caches is not None
    band_a = HIST_A + group
    band_b = HIST_B + group

    def tail_spec(keep, width):
        if keep >= tq:
            nblk = keep // tq
            return pl.BlockSpec((1, tq, width), lambda b, t: (b, jnp.clip(t - (n_tiles - nblk), 0, nblk - 1), 0))
        return pl.BlockSpec((1, keep, width), lambda b, t: (b, 0, 0))

    in_specs = [pl.BlockSpec((N_COLS, tq, LANES), lambda b, t: (0, b * n_tiles + t, 0)),
                pl.BlockSpec((1, tq, d), lambda b, t: (b, t, 0))]
    args = [z3, x]
    if has_cache:
        for c in caches:
            in_specs.append(pl.BlockSpec((1,) + c.shape[1:], lambda b, t: (b, 0, 0)))
            args.append(c)
    in_specs += [
        _resident(bias_a.shape),
        _resident(bias_b.shape),
        pl.BlockSpec(memory_space=pltpu.SMEM),
        _resident(wout_bf16.shape),
        _resident(gfin.shape),
    ]
    args += [bias_a, bias_b, sinks, wout_bf16, gfin]
    kern = functools.partial(
        _mixer_kernel, tq=tq, group=group,
        n_hist_a=HIST_A if has_cache else 0, n_hist_b=HIST_B if has_cache else 0,
        has_cache=has_cache, n_tiles=n_tiles, keep_a=keep_a, keep_b=keep_b)
    return pl.pallas_call(
        kern,
        grid=(nb, n_tiles),
        in_specs=in_specs,
        out_specs=[
            pl.BlockSpec((1, tq, d), lambda b, t: (b, t, 0)),
            tail_spec(keep_a, D_A), tail_spec(keep_a, D_A),
            tail_spec(keep_b, KV_B), tail_spec(keep_b, KV_B),
        ],
        out_shape=[
            jax.ShapeDtypeStruct((nb, s, d), F32),
            jax.ShapeDtypeStruct((nb, keep_a, D_A), F32), jax.ShapeDtypeStruct((nb, keep_a, D_A), F32),
            jax.ShapeDtypeStruct((nb, keep_b, KV_B), F32), jax.ShapeDtypeStruct((nb, keep_b, KV_B), F32),
        ],
        scratch_shapes=[
            pltpu.VMEM((N_PAIRS, HIST_A + tq, LANES), BF16),
            pltpu.VMEM((N_PAIRS, HIST_A + tq, LANES), BF16),
            pltpu.VMEM((N_KV_B, HIST_B + tq, LANES), BF16),
            pltpu.VMEM((N_KV_B, HIST_B + tq, LANES), BF16),
            pltpu.VMEM((N_SLOTS, 2 * group, band_a), F32),
            pltpu.VMEM((N_SLOTS, 2 * group, band_a), BF16),
            pltpu.VMEM((N_SLOTS, 2 * group, LANES), F32),
            pltpu.VMEM((N_SLOTS, 2 * group, LANES), F32),
            pltpu.VMEM((N_SLOTS, GROUP_B * group, band_b), F32),
            pltpu.VMEM((N_SLOTS, GROUP_B * group, band_b), BF16),
            pltpu.VMEM((N_SLOTS, GROUP_B * group, LANES), F32),
            pltpu.VMEM((N_SLOTS, GROUP_B * group, LANES), F32),
            pltpu.VMEM((tq, D_A + D_B), BF16),
            pltpu.VMEM((tq, d), F32),
        ],
        compiler_params=pltpu.CompilerParams(
            dimension_semantics=("arbitrary", "arbitrary"), vmem_limit_bytes=VMEM_LIMIT_BYTES),
        name=f"mixer_tq{tq}",
    )(*args)


def kernel(x_prompt, x_sample, cache_a_k, cache_a_v, cache_b_k, cache_b_v, norm_in, w_in, rel_bias_a,
           sinks_b, w_out, norm_final):
    b_p, seq, d = x_prompt.shape
    b_s, n_new, _ = x_sample.shape
    assert norm_in.shape[0] == 1, "one layer: the final norm is fused into the mixer call"
    assert n_new == CHUNK and seq % 256 == 0 and seq >= HIST_A
    assert cache_a_k.shape[2] == HIST_A and cache_b_k.shape[2] == HIST_B

    w_in_b = w_in[0].astype(BF16)
    w_out_b = w_out[0].astype(BF16)
    g_in = norm_in[0].reshape(1, d)
    g_fin = norm_final.reshape(1, d)
    sinks = sinks_b[0]

    z_p = _inproj(x_prompt.reshape(b_p * seq, d), g_in, w_in_b, BF16, tm=512)
    y_p, ak_p, av_p, bk_p, bv_p = _mixer(
        z_p, x_prompt, None, _rel_bias(rel_bias_a[0], 128), _alibi_bias(128), sinks, w_out_b, g_fin,
        tq=256, group=128, keep_a=HIST_A, keep_b=HIST_B)

    z_s = _inproj(x_sample.reshape(b_s * n_new, d), g_in, w_in_b, F32, tm=b_s * n_new)
    caches = (cache_a_k[0].reshape(b_s, HIST_A, D_A), cache_a_v[0].reshape(b_s, HIST_A, D_A),
              cache_b_k[0].reshape(b_s, HIST_B, KV_B), cache_b_v[0].reshape(b_s, HIST_B, KV_B))
    y_s, ak_s, av_s, bk_s, bv_s = _mixer(
        z_s, x_sample, caches, _rel_bias(rel_bias_a[0], CHUNK), _alibi_bias(CHUNK), sinks, w_out_b, g_fin,
        tq=CHUNK, group=CHUNK, keep_a=n_new, keep_b=n_new)

    def heads(rows, n_heads):
        return rows.reshape(1, rows.shape[0], rows.shape[1], n_heads, HEAD_DIM)

    def rolled(cache, new, n_heads):
        return jnp.concatenate([cache, heads(new, n_heads)], axis=2)[:, :, n_new:]

    return (y_p, y_s,
            heads(ak_p, N_HEADS), heads(av_p, N_HEADS), heads(bk_p, N_KV_B), heads(bv_p, N_KV_B),
            rolled(cache_a_k, ak_s, N_HEADS), rolled(cache_a_v, av_s, N_HEADS),
            rolled(cache_b_k, bk_s, N_KV_B), rolled(cache_b_v, bv_s, N_KV_B))
```
